```python
import jax, jax.numpy as jnp
from jax import lax
import numpy as np

D_MODEL = 1024
BATCH = 2
SEQ = 8192
DEPTH = 4
DEC_BATCH = 128
DEC_SEQ = 8
PAST_LEN = 8192
PAGE_SIZE = 128

D_MIX = D_MODEL
D_CONV = D_MIX // 2
N_HEADS = 8
HEAD_DIM = 64
N_KV_HEADS = 2
GROUP = N_HEADS // N_KV_HEADS
D_ATTN = N_HEADS * HEAD_DIM
D_KV = N_KV_HEADS * HEAD_DIM
D_IN = 2 * D_CONV + D_ATTN + 2 * D_KV
CONV_WIDTH = 31
WINDOW = 128
BLOCK = 128
ROPE_THETA = 10000.0
D_FF = 2816
N_EXPERTS = 8
TOP_K = 2
EPS = 1e-6

kernel_name = "hymba_conformer_swa_sink_moe_step"


def rmsnorm(x, g):
    xf = x.astype(jnp.float32)
    y = xf * lax.rsqrt(jnp.mean(xf * xf, -1, keepdims=True) + EPS)
    return (y * g.astype(jnp.float32)).astype(x.dtype)


def layernorm(x, g, b):
    xf = x.astype(jnp.float32)
    mu = jnp.mean(xf, -1, keepdims=True)
    xc = xf - mu
    y = xc * lax.rsqrt(jnp.mean(xc * xc, -1, keepdims=True) + EPS)
    return (y * g.astype(jnp.float32) + b.astype(jnp.float32)).astype(x.dtype)


def rope(x, pos):
    half = HEAD_DIM // 2
    inv = ROPE_THETA ** (-jnp.arange(half, dtype=jnp.float32) / half)
    ang = pos.astype(jnp.float32)[:, None] * inv[None, :]
    cos = jnp.cos(ang)[:, None, :]
    sin = jnp.sin(ang)[:, None, :]
    x1 = x[..., :half].astype(jnp.float32)
    x2 = x[..., half:].astype(jnp.float32)
    out = jnp.concatenate([x1 * cos - x2 * sin, x2 * cos + x1 * sin], -1)
    return out.astype(x.dtype)


def in_proj(h, w_in):
    B, S = h.shape[:2]
    p = jnp.einsum('bsd,de->bse', h, w_in)
    a_val, a_gate, q, k, v = jnp.split(
        p, [D_CONV, 2 * D_CONV, 2 * D_CONV + D_ATTN, 2 * D_CONV + D_ATTN + D_KV], -1)
    u = a_val * jax.nn.sigmoid(a_gate)
    q = q.reshape(B, S, N_HEADS, HEAD_DIM)
    k = k.reshape(B, S, N_KV_HEADS, HEAD_DIM)
    v = v.reshape(B, S, N_KV_HEADS, HEAD_DIM)
    return u, q, k, v


def conv_module(u_ext, conv_w, conv_b, ln_g, ln_b):
    y = lax.conv_general_dilated(
        u_ext, conv_w[:, None, :].astype(u_ext.dtype), window_strides=(1,), padding='VALID',
        dimension_numbers=('NWC', 'WIO', 'NWC'), feature_group_count=D_CONV)
    y = layernorm(y + conv_b, ln_g, ln_b)
    return jax.nn.silu(y)


def sink_probs(scores, mask, sinks):
    s = jnp.where(mask, scores, -jnp.inf)
    sk = sinks.astype(jnp.float32).reshape(N_KV_HEADS, GROUP)[:, :, None, None]
    m = jnp.maximum(jnp.max(s, -1, keepdims=True), sk)
    p = jnp.exp(s - m)
    return p / (jnp.sum(p, -1, keepdims=True) + jnp.exp(sk - m))


def swa_prompt(q, k, v, sinks):
    B, S = q.shape[:2]
    nb = S // BLOCK
    qb = q.reshape(B, nb, BLOCK, N_KV_HEADS, GROUP, HEAD_DIM)
    kp = jnp.concatenate([jnp.zeros_like(k[:, :BLOCK]), k], 1)
    vp = jnp.concatenate([jnp.zeros_like(v[:, :BLOCK]), v], 1)
    shp = (B, nb, BLOCK, N_KV_HEADS, HEAD_DIM)
    kb = jnp.concatenate([kp[:, :-BLOCK].reshape(shp), kp[:, BLOCK:].reshape(shp)], 2)
    vb = jnp.concatenate([vp[:, :-BLOCK].reshape(shp), vp[:, BLOCK:].reshape(shp)], 2)
    scores = jnp.einsum('bnqkgd,bnskd->bnkgqs', qb, kb).astype(jnp.float32) * (HEAD_DIM ** -0.5)
    i = jnp.arange(BLOCK)[:, None]
    j = jnp.arange(2 * BLOCK)[None, :]
    band = (j > i) & (j <= i + BLOCK)
    valid = (jnp.arange(nb)[:, None, None] > 0) | (j >= BLOCK)[None]
    mask = (band[None] & valid)[None, :, None, None]
    p = sink_probs(scores, mask, sinks)
    out = jnp.einsum('bnkgqs,bnskd->bnqkgd', p.astype(v.dtype), vb)
    return out.reshape(B, S, D_ATTN)


def swa_sample(q, k_ext, v_ext, q_pos, k_pos, sinks):
    DB, T = q.shape[:2]
    qg = q.reshape(DB, T, N_KV_HEADS, GROUP, HEAD_DIM)
    scores = jnp.einsum('bqkgd,bskd->bkgqs', qg, k_ext).astype(jnp.float32) * (HEAD_DIM ** -0.5)
    diff = q_pos[:, None] - k_pos[None, :]
    mask = (diff >= 0) & (diff < WINDOW)
    p = sink_probs(scores, mask, sinks)
    out = jnp.einsum('bkgqs,bskd->bqkgd', p.astype(v_ext.dtype), v_ext)
    return out.reshape(DB, T, D_ATTN)


def swiglu(h, w_gu, w_down):
    g, u = jnp.split(jnp.einsum('bsd,df->bsf', h, w_gu), 2, -1)
    return jnp.einsum('bsf,fd->bsd', jax.nn.silu(g) * u, w_down)


def moe_ffn(h, w_router, w_gu, w_down):
    logits = jnp.einsum('bsd,de->bse', h, w_router).astype(jnp.float32)
    vals, idx = lax.top_k(logits, TOP_K)
    wts = jax.nn.softmax(vals, -1)
    gate = jnp.sum(jax.nn.one_hot(idx, N_EXPERTS, dtype=jnp.float32) * wts[..., None], -2)
    gate = gate.astype(h.dtype)
    out = jnp.zeros_like(h)
    for e in range(N_EXPERTS):
        out = out + gate[..., e:e + 1] * swiglu(h, w_gu[e], w_down[e])
    return out


def channel_mixer(h, l, w_dense_gu, w_dense_down, w_router, w_moe_gu, w_moe_down):
    if l % 2 == 0:
        return swiglu(h, w_dense_gu[l // 2], w_dense_down[l // 2])
    return moe_ffn(h, w_router[l // 2], w_moe_gu[l // 2], w_moe_down[l // 2])


def setup_inputs(seed: int = 0) -> dict:
    key = jax.random.key(seed)
    ks = jax.random.split(key, 22)
    f32 = jnp.float32
    n_dense = (DEPTH + 1) // 2
    n_moe = DEPTH // 2
    swa_len = min(WINDOW, PAST_LEN)

    def nrm(k, shape, scale):
        return jax.random.normal(k, shape, f32) * scale

    return {
        "x_prompt": nrm(ks[0], (BATCH, SEQ, D_MODEL), 1.0),
        "x_sample": nrm(ks[1], (DEC_BATCH, DEC_SEQ, D_MODEL), 1.0),
        "cache_conv": nrm(ks[2], (DEPTH, DEC_BATCH, CONV_WIDTH - 1, D_CONV), 0.5),
        "cache_swa_k": nrm(ks[3], (DEPTH, DEC_BATCH, swa_len, N_KV_HEADS, HEAD_DIM), 1.0),
        "cache_swa_v": nrm(ks[4], (DEPTH, DEC_BATCH, swa_len, N_KV_HEADS, HEAD_DIM), 1.0),
        "g_mix": 1.0 + nrm(ks[5], (DEPTH, D_MODEL), 0.02),
        "w_in": nrm(ks[6], (DEPTH, D_MODEL, D_IN), D_MODEL ** -0.5),
        "conv_w": nrm(ks[7], (DEPTH, CONV_WIDTH, D_CONV), CONV_WIDTH ** -0.5),
        "conv_b": nrm(ks[8], (DEPTH, D_CONV), 0.02),
        "conv_ln_g": 1.0 + nrm(ks[9], (DEPTH, D_CONV), 0.02),
        "conv_ln_b": nrm(ks[10], (DEPTH, D_CONV), 0.02),
        "attn_sinks": nrm(ks[11], (DEPTH, N_HEADS), 0.5),
        "w_out": nrm(ks[12], (DEPTH, D_MIX, D_MODEL), D_MIX ** -0.5),
        "g_ffn": 1.0 + nrm(ks[13], (DEPTH, D_MODEL), 0.02),
        "w_dense_gu": nrm(ks[14], (n_dense, D_MODEL, 2 * D_FF), D_MODEL ** -0.5),
        "w_dense_down": nrm(ks[15], (n_dense, D_FF, D_MODEL), D_FF ** -0.5),
        "w_router": nrm(ks[16], (n_moe, D_MODEL, N_EXPERTS), D_MODEL ** -0.5),
        "w_moe_gu": nrm(ks[17], (n_moe, N_EXPERTS, D_MODEL, 2 * D_FF), D_MODEL ** -0.5),
        "w_moe_down": nrm(ks[18], (n_moe, N_EXPERTS, D_FF, D_MODEL), D_FF ** -0.5),
        "g_final": 1.0 + nrm(ks[19], (D_MODEL,), 0.02),
    }


def reference(x_prompt, x_sample, cache_conv, cache_swa_k, cache_swa_v, g_mix, w_in, conv_w,
              conv_b, conv_ln_g, conv_ln_b, attn_sinks, w_out, g_ffn, w_dense_gu, w_dense_down,
              w_router, w_moe_gu, w_moe_down, g_final):
    B, S = x_prompt.shape[:2]
    DB, T = x_sample.shape[:2]
    L = cache_swa_k.shape[2]
    pos_p = jnp.arange(S, dtype=jnp.int32)
    pos_s = PAST_LEN + jnp.arange(T, dtype=jnp.int32)
    kpos_s = jnp.concatenate([PAST_LEN - L + jnp.arange(L, dtype=jnp.int32), pos_s])
    n_keep = min(WINDOW, S)

    xp, xs = x_prompt, x_sample
    conv_p, k_p, v_p, conv_s, k_s, v_s = [], [], [], [], [], []
    for l in range(DEPTH):
        u, q, k, v = in_proj(rmsnorm(xp, g_mix[l]), w_in[l])
        q, k = rope(q, pos_p), rope(k, pos_p)
        u_ext = jnp.concatenate([jnp.zeros((B, CONV_WIDTH - 1, D_CONV), u.dtype), u], 1)
        c = conv_module(u_ext, conv_w[l], conv_b[l], conv_ln_g[l], conv_ln_b[l])
        a = swa_prompt(q, k, v, attn_sinks[l])
        xp = xp + jnp.einsum('bse,ed->bsd', jnp.concatenate([c, a], -1), w_out[l])
        conv_p.append(u_ext[:, -(CONV_WIDTH - 1):])
        k_p.append(k[:, S - n_keep:])
        v_p.append(v[:, S - n_keep:])

        u, q, k, v = in_proj(rmsnorm(xs, g_mix[l]), w_in[l])
        q, k = rope(q, pos_s), rope(k, pos_s)
        u_ext = jnp.concatenate([cache_conv[l], u], 1)
        k_ext = jnp.concatenate([cache_swa_k[l], k], 1)
        v_ext = jnp.concatenate([cache_swa_v[l], v], 1)
        c = conv_module(u_ext, conv_w[l], conv_b[l], conv_ln_g[l], conv_ln_b[l])
        a = swa_sample(q, k_ext, v_ext, pos_s, kpos_s, attn_sinks[l])
        xs = xs + jnp.einsum('bse,ed->bsd', jnp.concatenate([c, a], -1), w_out[l])
        conv_s.append(u_ext[:, -(CONV_WIDTH - 1):])
        k_s.append(k_ext[:, -L:])
        v_s.append(v_ext[:, -L:])

        xp = xp + channel_mixer(rmsnorm(xp, g_ffn[l]), l, w_dense_gu, w_dense_down,
                                w_router, w_moe_gu, w_moe_down)
        xs = xs + channel_mixer(rmsnorm(xs, g_ffn[l]), l, w_dense_gu, w_dense_down,
                                w_router, w_moe_gu, w_moe_down)

    y_prompt = rmsnorm(xp, g_final)
    y_sample = rmsnorm(xs, g_final)
    return (y_prompt, y_sample,
            jnp.stack(conv_p, 0), jnp.stack(k_p, 0), jnp.stack(v_p, 0),
            jnp.stack(conv_s, 0), jnp.stack(k_s, 0), jnp.stack(v_s, 0))
```

```python
import functools

import jax
import jax.numpy as jnp
from jax import lax
from jax.experimental import pallas as pl
from jax.experimental.pallas import tpu as pltpu

F32 = jnp.float32
BF16 = jnp.bfloat16
I32 = jnp.int32

D_MODEL = 1024
D_CONV = 512
N_HEADS = 8
HEAD_DIM = 64
N_KV_HEADS = 2
GROUP = N_HEADS // N_KV_HEADS
D_ATTN = N_HEADS * HEAD_DIM
D_KV = N_KV_HEADS * HEAD_DIM
D_IN = 2 * D_CONV + D_ATTN + 2 * D_KV
CONV_WIDTH = 31
WINDOW = 128
BLOCK = 128
ROPE_THETA = 10000.0
D_FF = 2816
N_EXPERTS = 8
EPS = 1e-6
PAST_LEN = 8192

LANES = 128
SUBLANES = 8
CONV_HALO = 32
CONV_PAD = CONV_HALO - (CONV_WIDTH - 1)
VMEM_LIMIT = 56 * 1024 * 1024


def _pick_tile(cands, *sizes):
    for c in cands:
        if all(s % c == 0 for s in sizes):
            return c
    raise ValueError(f"no tile in {cands} divides {sizes}")


def _params(*sem):
    return pltpu.CompilerParams(dimension_semantics=sem, vmem_limit_bytes=VMEM_LIMIT)


def _rms(x, g):
    return x * lax.rsqrt(jnp.mean(x * x, -1, keepdims=True) + EPS) * g


def _sigmoid(x):
    return 1.0 / (1.0 + jnp.exp(-x))


def _in_proj_kernel(x_ref, g_ref, w_ref, cos_ref, sin_ref, u_ref, q_ref, k_ref, v_ref):
    h = _rms(x_ref[...], g_ref[...]).astype(BF16)
    p = jnp.dot(h, w_ref[...], preferred_element_type=F32)
    u_ref[...] = p[:, :D_CONV] * _sigmoid(p[:, D_CONV:2 * D_CONV])
    cos = cos_ref[...]
    sin = sin_ref[...]
    lane = lax.broadcasted_iota(I32, cos.shape, 1)
    first_half = (lane % HEAD_DIM) < (HEAD_DIM // 2)

    def rope(xc):
        partner = jnp.where(first_half,
                            pltpu.roll(xc, LANES - HEAD_DIM // 2, 1),
                            pltpu.roll(xc, HEAD_DIM // 2, 1))
        return xc * cos + partner * sin

    q0 = 2 * D_CONV
    for c in range(D_ATTN // LANES):
        qc = rope(p[:, q0 + c * LANES:q0 + (c + 1) * LANES])
        q_ref[:, c * LANES:(c + 1) * LANES] = (qc * (HEAD_DIM ** -0.5)).astype(BF16)
    k0 = q0 + D_ATTN
    k_ref[...] = rope(p[:, k0:k0 + D_KV])
    v_ref[...] = p[:, k0 + D_KV:k0 + 2 * D_KV]


def _in_proj(x_all, g_mix3, w_in_bf, cos_t, sin_t, l, tile):
    n = x_all.shape[0]
    row = lambda i: (i, 0)
    return pl.pallas_call(
        _in_proj_kernel,
        grid=(n // tile,),
        in_specs=[
            pl.BlockSpec((tile, D_MODEL), row),
            pl.BlockSpec((None, 1, D_MODEL), lambda i: (l, 0, 0)),
            pl.BlockSpec((None, D_MODEL, D_IN), lambda i: (l, 0, 0)),
            pl.BlockSpec((tile, LANES), row),
            pl.BlockSpec((tile, LANES), row),
        ],
        out_specs=[
            pl.BlockSpec((tile, D_CONV), row),
            pl.BlockSpec((tile, D_ATTN), row),
            pl.BlockSpec((tile, D_KV), row),
            pl.BlockSpec((tile, D_KV), row),
        ],
        out_shape=[
            jax.ShapeDtypeStruct((n, D_CONV), F32),
            jax.ShapeDtypeStruct((n, D_ATTN), BF16),
            jax.ShapeDtypeStruct((n, D_KV), F32),
            jax.ShapeDtypeStruct((n, D_KV), F32),
        ],
        compiler_params=_params("parallel"),
        name="in_proj",
    )(x_all, g_mix3, w_in_bf, cos_t, sin_t)


def _ln_swish(y, g, b):
    mu = jnp.mean(y, -1, keepdims=True)
    yc = y - mu
    z = yc * lax.rsqrt(jnp.mean(yc * yc, -1, keepdims=True) + EPS) * g + b
    return z * _sigmoid(z)


def _conv_prompt_kernel(u_ref, w_ref, cb_ref, lg_ref, lb_ref, c_ref, ext_ref, y_ref, *, tile, rb):
    @pl.when(pl.program_id(1) == 0)
    def _():
        ext_ref[0:CONV_HALO, :] = jnp.zeros((CONV_HALO, D_CONV), F32)

    ext_ref[CONV_HALO:CONV_HALO + tile, :] = u_ref[...]
    for ct in range(D_CONV // LANES):
        cs = slice(ct * LANES, (ct + 1) * LANES)
        for r0 in range(0, tile, rb):
            acc = jnp.broadcast_to(cb_ref[:, cs], (rb, LANES))
            for b in range(SUBLANES):
                taps = [k for k in range(CONV_WIDTH) if (CONV_PAD + k) % SUBLANES == b]
                a_max = max((CONV_PAD + k) // SUBLANES for k in taps)
                sb = ext_ref[pl.ds(r0 + b, rb + SUBLANES * a_max), cs]
                for k in taps:
                    a = (CONV_PAD + k) // SUBLANES
                    acc = acc + sb[SUBLANES * a:SUBLANES * a + rb, :] * w_ref[pl.ds(k, 1), cs]
            y_ref[r0:r0 + rb, cs] = acc
    c_ref[...] = _ln_swish(y_ref[...], lg_ref[...], lb_ref[...]).astype(BF16)
    ext_ref[0:CONV_HALO, :] = ext_ref[tile:tile + CONV_HALO, :]


def _conv_prompt(u_all, conv_w, conv_b3, ln_g3, ln_b3, l, batch, seq, tile):
    nt = seq // tile
    vec = pl.BlockSpec((None, 1, D_CONV), lambda b, j: (l, 0, 0))
    return pl.pallas_call(
        functools.partial(_conv_prompt_kernel, tile=tile, rb=min(tile, 128)),
        grid=(batch, nt),
        in_specs=[
            pl.BlockSpec((tile, D_CONV), lambda b, j: (b * nt + j, 0)),
            pl.BlockSpec((None, CONV_WIDTH, D_CONV), lambda b, j: (l, 0, 0)),
            vec, vec, vec,
        ],
        out_specs=pl.BlockSpec((tile, D_CONV), lambda b, j: (b * nt + j, 0)),
        out_shape=jax.ShapeDtypeStruct((batch * seq, D_CONV), BF16),
        scratch_shapes=[pltpu.VMEM((tile + CONV_HALO, D_CONV), F32), pltpu.VMEM((tile, D_CONV), F32)],
        compiler_params=_params("arbitrary", "arbitrary"),
        name="conv_prompt",
    )(u_all, conv_w, conv_b3, ln_g3, ln_b3)


def _conv_sample_kernel(ext_ref, w_ref, cb_ref, lg_ref, lb_ref, c_ref, *, steps):
    for t in range(steps):
        acc = jnp.broadcast_to(cb_ref[...], ext_ref.shape[1:])
        for k in range(CONV_WIDTH):
            acc = acc + ext_ref[t + k] * w_ref[pl.ds(k, 1), :]
        c_ref[t] = _ln_swish(acc, lg_ref[...], lb_ref[...]).astype(BF16)


def _conv_sample(ext_t, conv_w, conv_b3, ln_g3, ln_b3, l, sb):
    rows, db, _ = ext_t.shape
    steps = rows - (CONV_WIDTH - 1)
    vec = pl.BlockSpec((None, 1, D_CONV), lambda i: (l, 0, 0))
    return pl.pallas_call(
        functools.partial(_conv_sample_kernel, steps=steps),
        grid=(db // sb,),
        in_specs=[
            pl.BlockSpec((rows, sb, D_CONV), lambda i: (0, i, 0)),
            pl.BlockSpec((None, CONV_WIDTH, D_CONV), lambda i: (l, 0, 0)),
            vec, vec, vec,
        ],
        out_specs=pl.BlockSpec((steps, sb, D_CONV), lambda i: (0, i, 0)),
        out_shape=jax.ShapeDtypeStruct((steps, db, D_CONV), BF16),
        compiler_params=_params("parallel"),
        name="conv_sample",
    )(ext_t, conv_w, conv_b3, ln_g3, ln_b3)


def _softmax_sink_pv(s, mask, sink, v_bf, dot_pv):
    s = jnp.where(mask, s, -jnp.inf)
    m = jnp.maximum(jnp.max(s, -1, keepdims=True), sink)
    p = jnp.exp(s - m)
    den = jnp.sum(p, -1, keepdims=True) + jnp.exp(sink - m)
    return dot_pv(p.astype(BF16), v_bf) / den


def _attn_prompt_kernel(sink_ref, q_ref, kp_ref, kc_ref, vp_ref, vc_ref, o_ref, *, l, tq):
    first = pl.program_id(1) == 0
    rows = GROUP * BLOCK
    qi = lax.broadcasted_iota(I32, (rows, 2 * BLOCK), 0) % BLOCK
    kj = lax.broadcasted_iota(I32, (rows, 2 * BLOCK), 1)
    band = (kj > qi) & (kj <= qi + BLOCK)
    head_of_row = lax.broadcasted_iota(I32, (rows, 1), 0) // BLOCK
    dot_qk = lambda a, b: lax.dot_general(a, b, (((1,), (1,)), ((), ())), preferred_element_type=F32)
    dot_pv = lambda a, b: jnp.dot(a, b, preferred_element_type=F32)
    for qb in range(tq // BLOCK):
        if qb == 0:
            kk = jnp.concatenate([kp_ref[...], kc_ref[0:BLOCK, :]], 0)
            vv = jnp.concatenate([vp_ref[...], vc_ref[0:BLOCK, :]], 0)
            mask = band & (kj >= jnp.where(first, BLOCK, 0))
        else:
            kk = kc_ref[(qb - 1) * BLOCK:(qb + 1) * BLOCK, :]
            vv = vc_ref[(qb - 1) * BLOCK:(qb + 1) * BLOCK, :]
            mask = band
        kk = kk.astype(BF16)
        vv = vv.astype(BF16)
        q = q_ref[qb * BLOCK:(qb + 1) * BLOCK, :]
        for g in range(N_KV_HEADS):
            hs = [g * GROUP + i for i in range(GROUP)]
            q4 = jnp.concatenate([q[:, h * HEAD_DIM:(h + 1) * HEAD_DIM] for h in hs], 0)
            sink = jnp.zeros((rows, 1), F32)
            for i, h in enumerate(hs):
                sink = jnp.where(head_of_row == i, sink_ref[l, h], sink)
            s = dot_qk(q4, kk[:, g * HEAD_DIM:(g + 1) * HEAD_DIM])
            o = _softmax_sink_pv(s, mask, sink, vv[:, g * HEAD_DIM:(g + 1) * HEAD_DIM], dot_pv)
            for i, h in enumerate(hs):
                o_ref[qb * BLOCK:(qb + 1) * BLOCK, h * HEAD_DIM:(h + 1) * HEAD_DIM] = (
                    o[i * BLOCK:(i + 1) * BLOCK, :].astype(BF16))


def _attn_prompt(sinks, q_all, k_all, v_all, l, batch, seq, tq):
    nq = seq // tq
    per = tq // BLOCK
    cur = lambda b, j: (b * nq + j, 0)
    prev = lambda b, j: (jnp.maximum((b * nq + j) * per - 1, 0), 0)
    return pl.pallas_call(
        functools.partial(_attn_prompt_kernel, l=l, tq=tq),
        grid=(batch, nq),
        in_specs=[
            pl.BlockSpec(memory_space=pltpu.SMEM),
            pl.BlockSpec((tq, D_ATTN), cur),
            pl.BlockSpec((BLOCK, D_KV), prev),
            pl.BlockSpec((tq, D_KV), cur),
            pl.BlockSpec((BLOCK, D_KV), prev),
            pl.BlockSpec((tq, D_KV), cur),
        ],
        out_specs=pl.BlockSpec((tq, D_ATTN), cur),
        out_shape=jax.ShapeDtypeStruct((batch * seq, D_ATTN), BF16),
        compiler_params=_params("parallel", "parallel"),
        name="attn_prompt",
    )(sinks, q_all, k_all, k_all, v_all, v_all)


def _attn_sample_kernel(sink_ref, q_ref, kn_ref, vn_ref, kc_ref, vc_ref, o_ref, *, l, steps):
    sb = q_ref.shape[0]
    hist = kc_ref.shape[1]
    rows = GROUP * steps
    pad = jnp.zeros((sb, hist - steps, D_KV), F32)
    kk = jnp.concatenate([kc_ref[...], kn_ref[...], pad], 1).astype(BF16)
    vv = jnp.concatenate([vc_ref[...], vn_ref[...], pad], 1).astype(BF16)
    t = lax.broadcasted_iota(I32, (sb, rows, 2 * hist), 1) % steps
    j = lax.broadcasted_iota(I32, (sb, rows, 2 * hist), 2)
    mask = ((j < hist) & (t + hist - j < WINDOW)) | ((j >= hist) & (j - hist <= t) & (t - (j - hist) < WINDOW))
    head_of_row = lax.broadcasted_iota(I32, (rows, 1), 0) // steps
    dot_qk = lambda a, b: jnp.einsum("bqd,bkd->bqk", a, b, preferred_element_type=F32)
    dot_pv = lambda a, b: jnp.einsum("bqk,bkd->bqd", a, b, preferred_element_type=F32)
    q = q_ref[...]
    for g in range(N_KV_HEADS):
        hs = [g * GROUP + i for i in range(GROUP)]
        q4 = jnp.concatenate([q[:, :, h * HEAD_DIM:(h + 1) * HEAD_DIM] for h in hs], 1)
        sink = jnp.zeros((rows, 1), F32)
        for i, h in enumerate(hs):
            sink = jnp.where(head_of_row == i, sink_ref[l, h], sink)
        s = dot_qk(q4, kk[:, :, g * HEAD_DIM:(g + 1) * HEAD_DIM])
        o = _softmax_sink_pv(s, mask, sink[None], vv[:, :, g * HEAD_DIM:(g + 1) * HEAD_DIM], dot_pv)
        for i, h in enumerate(hs):
            o_ref[:, :, h * HEAD_DIM:(h + 1) * HEAD_DIM] = o[:, i * steps:(i + 1) * steps, :].astype(BF16)


def _attn_sample(sinks, q_s, k_new, v_new, cache_k4, cache_v4, l, sb):
    db, steps, _ = q_s.shape
    hist = cache_k4.shape[2]
    seq3 = lambda i: (i, 0, 0)
    return pl.pallas_call(
        functools.partial(_attn_sample_kernel, l=l, steps=steps),
        grid=(db // sb,),
        in_specs=[
            pl.BlockSpec(memory_space=pltpu.SMEM),
            pl.BlockSpec((sb, steps, D_ATTN), seq3),
            pl.BlockSpec((sb, steps, D_KV), seq3),
            pl.BlockSpec((sb, steps, D_KV), seq3),
            pl.BlockSpec((None, sb, hist, D_KV), lambda i: (l, i, 0, 0)),
            pl.BlockSpec((None, sb, hist, D_KV), lambda i: (l, i, 0, 0)),
        ],
        out_specs=pl.BlockSpec((sb, steps, D_ATTN), seq3),
        out_shape=jax.ShapeDtypeStruct((db, steps, D_ATTN), BF16),
        compiler_params=_params("parallel"),
        name="attn_sample",
    )(sinks, q_s, k_new, v_new, cache_k4, cache_v4)


def _out_proj_kernel(c_ref, a_ref, w_ref, x_ref, o_ref):
    o_ref[...] = (x_ref[...]
                  + jnp.dot(c_ref[...], w_ref[0:D_CONV, :], preferred_element_type=F32)
                  + jnp.dot(a_ref[...], w_ref[D_CONV:, :], preferred_element_type=F32))


def _out_proj(c, a, w_out_bf, x_all, l, row0, tile):
    rows = c.shape[0]
    off = row0 // tile
    return pl.pallas_call(
        _out_proj_kernel,
        grid=(rows // tile,),
        in_specs=[
            pl.BlockSpec((tile, D_CONV), lambda i: (i, 0)),
            pl.BlockSpec((tile, D_ATTN), lambda i: (i, 0)),
            pl.BlockSpec((None, D_MODEL, D_MODEL), lambda i: (l, 0, 0)),
            pl.BlockSpec((tile, D_MODEL), lambda i: (off + i, 0)),
        ],
        out_specs=pl.BlockSpec((tile, D_MODEL), lambda i: (off + i, 0)),
        out_shape=jax.ShapeDtypeStruct(x_all.shape, F32),
        input_output_aliases={3: 0},
        compiler_params=_params("parallel"),
        name="out_proj",
    )(c, a, w_out_bf, x_all)


def _ffn_dense_kernel(x_ref, g_ref, wgu_ref, wd_ref, o_ref, act_ref, *, chunk):
    x = x_ref[...]
    h = _rms(x, g_ref[...]).astype(BF16)
    for c0 in range(0, D_FF, chunk):
        gate = jnp.dot(h, wgu_ref[:, c0:c0 + chunk], preferred_element_type=F32)
        up = jnp.dot(h, wgu_ref[:, D_FF + c0:D_FF + c0 + chunk], preferred_element_type=F32)
        act_ref[:, c0:c0 + chunk] = (gate * _sigmoid(gate) * up).astype(BF16)
    o_ref[...] = x + jnp.dot(act_ref[...], wd_ref[...], preferred_element_type=F32)


def _ffn_dense(x_all, g_ffn3, wgu_bf, wd_bf, l, tile):
    n = x_all.shape[0]
    once = pl.Buffered(1)
    return pl.pallas_call(
        functools.partial(_ffn_dense_kernel, chunk=256),
        grid=(n // tile,),
        in_specs=[
            pl.BlockSpec((tile, D_MODEL), lambda i: (i, 0)),
            pl.BlockSpec((None, 1, D_MODEL), lambda i: (l, 0, 0)),
            pl.BlockSpec((None, D_MODEL, 2 * D_FF), lambda i: (l // 2, 0, 0), pipeline_mode=once),
            pl.BlockSpec((None, D_FF, D_MODEL), lambda i: (l // 2, 0, 0), pipeline_mode=once),
        ],
        out_specs=pl.BlockSpec((tile, D_MODEL), lambda i: (i, 0)),
        out_shape=jax.ShapeDtypeStruct(x_all.shape, F32),
        scratch_shapes=[pltpu.VMEM((tile, D_FF), BF16)],
        input_output_aliases={0: 0},
        compiler_params=_params("parallel"),
        name="ffn_dense",
    )(x_all, g_ffn3, wgu_bf, wd_bf)


INFO_E, INFO_G, INFO_R = 0, 2, 4


def _route_kernel(x_ref, g_ref, wh_ref, wl_ref, info_ref, cnt_ref, run_ref):
    @pl.when(pl.program_id(0) == 0)
    def _():
        run_ref[...] = jnp.zeros_like(run_ref)

    h = _rms(x_ref[...], g_ref[...])
    h_hi = h.astype(BF16)
    h_lo = (h - h_hi.astype(F32)).astype(BF16)
    dot = lambda a, b: jnp.dot(a, b, preferred_element_type=F32)
    logits = dot(h_hi, wh_ref[...]) + dot(h_hi, wl_ref[...]) + dot(h_lo, wh_ref[...])
    t = logits.shape[0]
    lane = lax.broadcasted_iota(I32, logits.shape, 1)
    lg = jnp.where(lane < N_EXPERTS, logits, -jnp.inf)
    m1 = jnp.max(lg, -1, keepdims=True)
    i1 = jnp.min(jnp.where(lg == m1, lane, LANES), -1, keepdims=True)
    lg2 = jnp.where(lane == i1, -jnp.inf, lg)
    m2 = jnp.max(lg2, -1, keepdims=True)
    i2 = jnp.min(jnp.where(lg2 == m2, lane, LANES), -1, keepdims=True)
    e = jnp.exp(m2 - m1)
    g1 = 1.0 / (1.0 + e)
    g2 = e / (1.0 + e)
    sel1 = lane == i1
    sel2 = lane == i2
    onehot = jnp.where(sel1 | sel2, 1.0, 0.0)
    r = lax.broadcasted_iota(I32, (t, t), 0)
    c = lax.broadcasted_iota(I32, (t, t), 1)
    tri = jnp.where(r > c, 1.0, 0.0).astype(BF16)
    before = dot(tri, onehot.astype(BF16)) + run_ref[0:1, :]
    r1 = jnp.sum(jnp.where(sel1, before, 0.0), -1, keepdims=True)
    r2 = jnp.sum(jnp.where(sel2, before, 0.0), -1, keepdims=True)
    run_ref[...] = run_ref[...] + jnp.sum(onehot, 0, keepdims=True)
    cnt_ref[...] = run_ref[...]
    info = jnp.zeros(logits.shape, F32)
    for pos, val in ((INFO_E, i1.astype(F32)), (INFO_E + 1, i2.astype(F32)), (INFO_G, g1), (INFO_G + 1, g2),
                     (INFO_R, r1), (INFO_R + 1, r2)):
        info = jnp.where(lane == pos, val, info)
    info_ref[...] = info


def _route(x_all, g_ffn3, wr_hi, wr_lo, l, tile):
    n = x_all.shape[0]
    const = lambda i: (l // 2, 0, 0)
    return pl.pallas_call(
        _route_kernel,
        grid=(n // tile,),
        in_specs=[
            pl.BlockSpec((tile, D_MODEL), lambda i: (i, 0)),
            pl.BlockSpec((None, 1, D_MODEL), lambda i: (l, 0, 0)),
            pl.BlockSpec((None, D_MODEL, LANES), const),
            pl.BlockSpec((None, D_MODEL, LANES), const),
        ],
        out_specs=[
            pl.BlockSpec((tile, LANES), lambda i: (i, 0)),
            pl.BlockSpec((SUBLANES, LANES), lambda i: (0, 0)),
        ],
        out_shape=[
            jax.ShapeDtypeStruct((n, LANES), F32),
            jax.ShapeDtypeStruct((SUBLANES, LANES), F32),
        ],
        scratch_shapes=[pltpu.VMEM((SUBLANES, LANES), F32)],
        compiler_params=_params("arbitrary"),
        name="route",
    )(x_all, g_ffn3, wr_hi, wr_lo)


def _row_copy(src_ref, src_row, dst_ref, dst_row, sem):
    return pltpu.make_async_copy(src_ref.at[pl.ds(src_row, 1)], dst_ref.at[pl.ds(dst_row, 1)], sem)


def _dispatch_kernel(pos_ref, x_ref, xs_in_ref, xs_ref, sem, *, tile):
    del xs_in_ref

    def body(i, carry):
        for j in range(2):
            _row_copy(x_ref, i, xs_ref, pos_ref[0, j, i], sem.at[j]).start()
        return carry

    lax.fori_loop(0, tile, body, 0)
    for j in range(2):
        pltpu.make_async_copy(x_ref, xs_ref.at[pl.ds(0, tile)], sem.at[j]).wait()


def _dispatch(pos3, x_all, xs_zero, tile):
    n = x_all.shape[0]
    return pl.pallas_call(
        functools.partial(_dispatch_kernel, tile=tile),
        grid=(n // tile,),
        in_specs=[
            pl.BlockSpec((1, 2, tile), lambda i: (i, 0, 0), memory_space=pltpu.SMEM),
            pl.BlockSpec((tile, D_MODEL), lambda i: (i, 0)),
            pl.BlockSpec(memory_space=pl.ANY),
        ],
        out_specs=pl.BlockSpec(memory_space=pl.ANY),
        out_shape=jax.ShapeDtypeStruct(xs_zero.shape, F32),
        scratch_shapes=[pltpu.SemaphoreType.DMA((2,))],
        input_output_aliases={2: 0},
        compiler_params=_params("arbitrary"),
        name="moe_dispatch",
    )(pos3, x_all, xs_zero)


def _moe_kernel(te_ref, used_ref, xs_ref, g_ref, wg_ref, wu_ref, wd_ref, y_ref, h_ref):
    t = pl.program_id(0)
    j = pl.program_id(1)
    used = t < used_ref[0]

    @pl.when(used & (j == 0))
    def _():
        h_ref[...] = _rms(xs_ref[...], g_ref[...]).astype(BF16)

    @pl.when(used)
    def _():
        h = h_ref[...]
        gate = jnp.dot(h, wg_ref[...].astype(BF16), preferred_element_type=F32)
        up = jnp.dot(h, wu_ref[...].astype(BF16), preferred_element_type=F32)
        act = (gate * _sigmoid(gate) * up).astype(BF16)
        part = jnp.dot(act, wd_ref[...].astype(BF16), preferred_element_type=F32)

        @pl.when(j == 0)
        def _():
            y_ref[...] = part

        @pl.when(j > 0)
        def _():
            y_ref[...] = y_ref[...] + part

    @pl.when(jnp.logical_not(used) & (j == 0))
    def _():
        y_ref[...] = jnp.zeros_like(y_ref)


def _moe(tile_expert, n_used, xs, g_ffn3, w_gu, w_down, l, tm, chunk):
    nt = xs.shape[0] // tm
    nj = D_FF // chunk
    m = l // 2

    def tile_idx(t, used):
        return jnp.minimum(t, used[0] - 1)

    def chunk_idx(t, j, used):
        return jnp.where(t < used[0], j, nj - 1)

    grid_spec = pltpu.PrefetchScalarGridSpec(
        num_scalar_prefetch=2,
        grid=(nt, nj),
        in_specs=[
            pl.BlockSpec((tm, D_MODEL), lambda t, j, te, used: (tile_idx(t, used), 0)),
            pl.BlockSpec((None, 1, D_MODEL), lambda t, j, te, used: (l, 0, 0)),
            pl.BlockSpec((None, None, D_MODEL, chunk),
                         lambda t, j, te, used: (m, te[tile_idx(t, used)], 0, chunk_idx(t, j, used))),
            pl.BlockSpec((None, None, D_MODEL, chunk),
                         lambda t, j, te, used: (m, te[tile_idx(t, used)], 0, nj + chunk_idx(t, j, used))),
            pl.BlockSpec((None, None, chunk, D_MODEL),
                         lambda t, j, te, used: (m, te[tile_idx(t, used)], chunk_idx(t, j, used), 0)),
        ],
        out_specs=pl.BlockSpec((tm, D_MODEL), lambda t, j, te, used: (t, 0)),
        scratch_shapes=[pltpu.VMEM((tm, D_MODEL), BF16)],
    )
    return pl.pallas_call(
        _moe_kernel,
        grid_spec=grid_spec,
        out_shape=jax.ShapeDtypeStruct(xs.shape, F32),
        compiler_params=_params("arbitrary", "arbitrary"),
        name="moe_experts",
    )(tile_expert, n_used, xs, g_ffn3, w_gu, w_gu, w_down)


def _combine_kernel(pos_ref, info_ref, x_ref, y_ref, o_ref, buf_ref, sem, *, tile):
    def body(i, carry):
        for j in range(2):
            _row_copy(y_ref, pos_ref[0, j, i], buf_ref.at[j], i, sem.at[j]).start()
        return carry

    lax.fori_loop(0, tile, body, 0)
    for j in range(2):
        pltpu.make_async_copy(y_ref.at[pl.ds(0, tile)], buf_ref.at[j], sem.at[j]).wait()
    info = info_ref[...]
    o_ref[...] = (x_ref[...]
                  + info[:, INFO_G:INFO_G + 1] * buf_ref[0]
                  + info[:, INFO_G + 1:INFO_G + 2] * buf_ref[1])


def _combine(pos3, info, x_all, y, tile):
    n = x_all.shape[0]
    return pl.pallas_call(
        functools.partial(_combine_kernel, tile=tile),
        grid=(n // tile,),
        in_specs=[
            pl.BlockSpec((1, 2, tile), lambda i: (i, 0, 0), memory_space=pltpu.SMEM),
            pl.BlockSpec((tile, LANES), lambda i: (i, 0)),
            pl.BlockSpec((tile, D_MODEL), lambda i: (i, 0)),
            pl.BlockSpec(memory_space=pl.ANY),
        ],
        out_specs=pl.BlockSpec((tile, D_MODEL), lambda i: (i, 0)),
        out_shape=jax.ShapeDtypeStruct(x_all.shape, F32),
        scratch_shapes=[pltpu.VMEM((2, tile, D_MODEL), F32), pltpu.SemaphoreType.DMA((2,))],
        input_output_aliases={2: 0},
        compiler_params=_params("arbitrary"),
        name="moe_combine",
    )(pos3, info, x_all, y)


def _ffn_moe(x_all, g_ffn3, wr_hi, wr_lo, w_moe_gu, w_moe_down, l, tile, tm):
    n = x_all.shape[0]
    info, cnt = _route(x_all, g_ffn3, wr_hi, wr_lo, l, tile)
    counts = cnt[0, :N_EXPERTS].astype(I32)
    tiles_per = (counts + tm - 1) // tm
    tile_end = jnp.cumsum(tiles_per)
    base = (tile_end - tiles_per) * tm
    experts = info[:, INFO_E:INFO_E + 2].astype(I32)
    ranks = info[:, INFO_R:INFO_R + 2].astype(I32)
    pos = jnp.sum(jnp.where(experts[..., None] == jnp.arange(N_EXPERTS), base, 0), -1) + ranks
    n_tiles = (2 * n + N_EXPERTS * (tm - 1)) // tm
    n_used = tile_end[-1:]
    tile_expert = jnp.minimum(jnp.sum(jnp.arange(n_tiles)[:, None] >= tile_end[None, :], -1), N_EXPERTS - 1)
    td = _pick_tile((256, 128), n)
    pos3 = pos.T.reshape(2, n // td, td).transpose(1, 0, 2)
    xs = _dispatch(pos3, x_all, jnp.zeros((n_tiles * tm, D_MODEL), F32), td)
    y = _moe(tile_expert.astype(I32), n_used.astype(I32), xs, g_ffn3, w_moe_gu, w_moe_down, l, tm, 256)
    return _combine(pos3, info, x_all, y, td)


def _final_kernel(x_ref, g_ref, o_ref):
    o_ref[...] = _rms(x_ref[...], g_ref[...])


def _final_norm(x_all, g2, row0, rows, tile):
    off = row0 // tile
    return pl.pallas_call(
        _final_kernel,
        grid=(rows // tile,),
        in_specs=[
            pl.BlockSpec((tile, D_MODEL), lambda i: (off + i, 0)),
            pl.BlockSpec((1, D_MODEL), lambda i: (0, 0)),
        ],
        out_specs=pl.BlockSpec((tile, D_MODEL), lambda i: (i, 0)),
        out_shape=jax.ShapeDtypeStruct((rows, D_MODEL), F32),
        compiler_params=_params("parallel"),
        name="final_norm",
    )(x_all, g2)


def _rope_tables(pos):
    half = HEAD_DIM // 2
    inv = ROPE_THETA ** (-jnp.arange(half, dtype=F32) / half)
    ang = pos.astype(F32)[:, None] * inv[None, :]
    cos = jnp.cos(ang)
    sin = jnp.sin(ang)
    reps = LANES // HEAD_DIM
    return jnp.tile(jnp.concatenate([cos, cos], -1), (1, reps)), jnp.tile(jnp.concatenate([-sin, sin], -1), (1, reps))


def kernel(x_prompt, x_sample, cache_conv, cache_swa_k, cache_swa_v, g_mix, w_in, conv_w, conv_b, conv_ln_g,
           conv_ln_b, attn_sinks, w_out, g_ffn, w_dense_gu, w_dense_down, w_router, w_moe_gu, w_moe_down, g_final):
    batch, seq, _ = x_prompt.shape
    db, steps, _ = x_sample.shape
    depth = g_mix.shape[0]
    hist = cache_swa_k.shape[2]
    n_p, n_s = batch * seq, db * steps
    n = n_p + n_s
    assert seq % BLOCK == 0 and seq >= CONV_WIDTH - 1 and steps <= hist and steps < CONV_WIDTH - 1
    assert hist == min(WINDOW, PAST_LEN)
    tile = _pick_tile((512, 256, 128), n_p, n_s)
    tq = _pick_tile((512, 128), seq)
    tm = _pick_tile((1024, 512, 256), 2 * n)
    sb_attn = _pick_tile((16, 8), db)
    sb_conv = _pick_tile((64, 8), db)

    x_all = jnp.concatenate([x_prompt.reshape(n_p, D_MODEL), x_sample.reshape(n_s, D_MODEL)], 0)
    pos_all = jnp.concatenate([jnp.tile(jnp.arange(seq, dtype=I32), batch),
                               jnp.tile(PAST_LEN + jnp.arange(steps, dtype=I32), db)])
    cos_t, sin_t = _rope_tables(pos_all)

    vec3 = lambda a: a.reshape(a.shape[0], 1, a.shape[1])
    g_mix3, g_ffn3, conv_b3, ln_g3, ln_b3 = map(vec3, (g_mix, g_ffn, conv_b, conv_ln_g, conv_ln_b))
    w_in_bf, w_out_bf = w_in.astype(BF16), w_out.astype(BF16)
    wgu_bf, wd_bf = w_dense_gu.astype(BF16), w_dense_down.astype(BF16)
    wr = jnp.pad(w_router, ((0, 0), (0, 0), (0, LANES - N_EXPERTS)))
    wr_hi = wr.astype(BF16)
    wr_lo = (wr - wr_hi.astype(F32)).astype(BF16)
    cache_k4 = cache_swa_k.reshape(depth, db, hist, D_KV)
    cache_v4 = cache_swa_v.reshape(depth, db, hist, D_KV)

    conv_p, k_p, v_p, conv_s, k_s, v_s = [], [], [], [], [], []
    for l in range(depth):
        u, q, k, v = _in_proj(x_all, g_mix3, w_in_bf, cos_t, sin_t, l, tile)

        c_p = _conv_prompt(u, conv_w, conv_b3, ln_g3, ln_b3, l, batch, seq, tq)
        a_p = _attn_prompt(attn_sinks, q, k, v, l, batch, seq, tq)
        x_all = _out_proj(c_p, a_p, w_out_bf, x_all, l, 0, tile)

        u_s = u[n_p:].reshape(db, steps, D_CONV)
        k_new = k[n_p:].reshape(db, steps, D_KV)
        v_new = v[n_p:].reshape(db, steps, D_KV)
        u_ext = jnp.concatenate([cache_conv[l], u_s], 1)
        c_t = _conv_sample(u_ext.transpose(1, 0, 2), conv_w, conv_b3, ln_g3, ln_b3, l, sb_conv)
        c_s = c_t.transpose(1, 0, 2).reshape(n_s, D_CONV)
        a_s = _attn_sample(attn_sinks, q[n_p:].reshape(db, steps, D_ATTN), k_new, v_new, cache_k4, cache_v4,
                           l, sb_attn).reshape(n_s, D_ATTN)
        x_all = _out_proj(c_s, a_s, w_out_bf, x_all, l, n_p, tile)

        conv_p.append(u[:n_p].reshape(batch, seq, D_CONV)[:, seq - (CONV_WIDTH - 1):])
        k_p.append(k[:n_p].reshape(batch, seq, N_KV_HEADS, HEAD_DIM)[:, seq - min(WINDOW, seq):])
        v_p.append(v[:n_p].reshape(batch, seq, N_KV_HEADS, HEAD_DIM)[:, seq - min(WINDOW, seq):])
        conv_s.append(u_ext[:, steps:])
        k_s.append(jnp.concatenate([cache_swa_k[l][:, steps:], k_new.reshape(db, steps, N_KV_HEADS, HEAD_DIM)], 1))
        v_s.append(jnp.concatenate([cache_swa_v[l][:, steps:], v_new.reshape(db, steps, N_KV_HEADS, HEAD_DIM)], 1))

        if l % 2 == 0:
            x_all = _ffn_dense(x_all, g_ffn3, wgu_bf, wd_bf, l, tile)
        else:
            x_all = _ffn_moe(x_all, g_ffn3, wr_hi, wr_lo, w_moe_gu, w_moe_down, l, tile, tm)

    g_fin = g_final.reshape(1, D_MODEL)
    y_prompt = _final_norm(x_all, g_fin, 0, n_p, tile).reshape(batch, seq, D_MODEL)
    y_sample = _final_norm(x_all, g_fin, n_p, n_s, tile).reshape(db, steps, D_MODEL)
    return (y_prompt, y_sample, jnp.stack(conv_p, 0), jnp.stack(k_p, 0), jnp.stack(v_p, 0),
            jnp.stack(conv_s, 0), jnp.stack(k_s, 0), jnp.stack(v_s, 0))
```

```python
import functools

import jax
import jax.numpy as jnp
from jax import lax
from jax.experimental import pallas as pl
from jax.experimental.pallas import tpu as pltpu

F32 = jnp.float32
BF16 = jnp.bfloat16
I32 = jnp.int32

D_MODEL = 1024
D_CONV = 512
N_HEADS = 8
HEAD_DIM = 64
N_KV_HEADS = 2
GROUP = N_HEADS // N_KV_HEADS
D_ATTN = N_HEADS * HEAD_DIM
D_KV = N_KV_HEADS * HEAD_DIM
D_IN = 2 * D_CONV + D_ATTN + 2 * D_KV
CONV_WIDTH = 31
WINDOW = 128
BLOCK = 128
ROPE_THETA = 10000.0
D_FF = 2816
N_EXPERTS = 8
EPS = 1e-6
PAST_LEN = 8192

LANES = 128
SUBLANES = 8
CONV_HALO = 32
CONV_PAD = CONV_HALO - (CONV_WIDTH - 1)
VMEM_LIMIT = 56 * 1024 * 1024


def _pick_tile(cands, *sizes):
    for c in cands:
        if all(s % c == 0 for s in sizes):
            return c
    raise ValueError(f"no tile in {cands} divides {sizes}")


def _params(*sem):
    return pltpu.CompilerParams(dimension_semantics=sem, vmem_limit_bytes=VMEM_LIMIT)


def _rms(x, g):
    return x * lax.rsqrt(jnp.mean(x * x, -1, keepdims=True) + EPS) * g


def _sigmoid(x):
    return 1.0 / (1.0 + jnp.exp(-x))


def _in_proj_kernel(x_ref, g_ref, w_ref, cos_ref, sin_ref, u_ref, q_ref, k_ref, v_ref):
    h = _rms(x_ref[...], g_ref[...]).astype(BF16)
    p = jnp.dot(h, w_ref[...], preferred_element_type=F32)
    u_ref[...] = p[:, :D_CONV] * _sigmoid(p[:, D_CONV:2 * D_CONV])
    cos = cos_ref[...]
    sin = sin_ref[...]
    lane = lax.broadcasted_iota(I32, cos.shape, 1)
    first_half = (lane % HEAD_DIM) < (HEAD_DIM // 2)

    def rope(xc):
        partner = jnp.where(first_half,
                            pltpu.roll(xc, LANES - HEAD_DIM // 2, 1),
                            pltpu.roll(xc, HEAD_DIM // 2, 1))
        return xc * cos + partner * sin

    q0 = 2 * D_CONV
    for c in range(D_ATTN // LANES):
        qc = rope(p[:, q0 + c * LANES:q0 + (c + 1) * LANES])
        q_ref[:, c * LANES:(c + 1) * LANES] = (qc * (HEAD_DIM ** -0.5)).astype(BF16)
    k0 = q0 + D_ATTN
    k_ref[...] = rope(p[:, k0:k0 + D_KV])
    v_ref[...] = p[:, k0 + D_KV:k0 + 2 * D_KV]


def _in_proj(x_all, g_mix3, w_in_bf, cos_t, sin_t, l, tile):
    n = x_all.shape[0]
    row = lambda i: (i, 0)
    return pl.pallas_call(
        _in_proj_kernel,
        grid=(n // tile,),
        in_specs=[
            pl.BlockSpec((tile, D_MODEL), row),
            pl.BlockSpec((None, 1, D_MODEL), lambda i: (l, 0, 0)),
            pl.BlockSpec((None, D_MODEL, D_IN), lambda i: (l, 0, 0)),
            pl.BlockSpec((tile, LANES), row),
            pl.BlockSpec((tile, LANES), row),
        ],
        out_specs=[
            pl.BlockSpec((tile, D_CONV), row),
            pl.BlockSpec((tile, D_ATTN), row),
            pl.BlockSpec((tile, D_KV), row),
            pl.BlockSpec((tile, D_KV), row),
        ],
        out_shape=[
            jax.ShapeDtypeStruct((n, D_CONV), F32),
            jax.ShapeDtypeStruct((n, D_ATTN), BF16),
            jax.ShapeDtypeStruct((n, D_KV), F32),
            jax.ShapeDtypeStruct((n, D_KV), F32),
        ],
        compiler_params=_params("parallel"),
        name="in_proj",
    )(x_all, g_mix3, w_in_bf, cos_t, sin_t)


def _ln_swish(y, g, b):
    mu = jnp.mean(y, -1, keepdims=True)
    yc = y - mu
    z = yc * lax.rsqrt(jnp.mean(yc * yc, -1, keepdims=True) + EPS) * g + b
    return z * _sigmoid(z)


def _conv_prompt_kernel(u_ref, w_ref, cb_ref, lg_ref, lb_ref, c_ref, ext_ref, y_ref, *, tile, rb):
    @pl.when(pl.program_id(1) == 0)
    def _():
        ext_ref[0:CONV_HALO, :] = jnp.zeros((CONV_HALO, D_CONV), F32)

    ext_ref[CONV_HALO:CONV_HALO + tile, :] = u_ref[...]
    for ct in range(D_CONV // LANES):
        cs = slice(ct * LANES, (ct + 1) * LANES)
        for r0 in range(0, tile, rb):
            acc = jnp.broadcast_to(cb_ref[:, cs], (rb, LANES))
            for b in range(SUBLANES):
                taps = [k for k in range(CONV_WIDTH) if (CONV_PAD + k) % SUBLANES == b]
                a_max = max((CONV_PAD + k) // SUBLANES for k in taps)
                sb = ext_ref[pl.ds(r0 + b, rb + SUBLANES * a_max), cs]
                for k in taps:
                    a = (CONV_PAD + k) // SUBLANES
                    acc = acc + sb[SUBLANES * a:SUBLANES * a + rb, :] * w_ref[pl.ds(k, 1), cs]
            y_ref[r0:r0 + rb, cs] = acc
    c_ref[...] = _ln_swish(y_ref[...], lg_ref[...], lb_ref[...]).astype(BF16)
    ext_ref[0:CONV_HALO, :] = ext_ref[tile:tile + CONV_HALO, :]


def _conv_prompt(u_all, conv_w, conv_b3, ln_g3, ln_b3, l, batch, seq, tile):
    nt = seq // tile
    vec = pl.BlockSpec((None, 1, D_CONV), lambda b, j: (l, 0, 0))
    return pl.pallas_call(
        functools.partial(_conv_prompt_kernel, tile=tile, rb=min(tile, 128)),
        grid=(batch, nt),
        in_specs=[
            pl.BlockSpec((tile, D_CONV), lambda b, j: (b * nt + j, 0)),
            pl.BlockSpec((None, CONV_WIDTH, D_CONV), lambda b, j: (l, 0, 0)),
            vec, vec, vec,
        ],
        out_specs=pl.BlockSpec((tile, D_CONV), lambda b, j: (b * nt + j, 0)),
        out_shape=jax.ShapeDtypeStruct((batch * seq, D_CONV), BF16),
        scratch_shapes=[pltpu.VMEM((tile + CONV_HALO, D_CONV), F32), pltpu.VMEM((tile, D_CONV), F32)],
        compiler_params=_params("arbitrary", "arbitrary"),
        name="conv_prompt",
    )(u_all, conv_w, conv_b3, ln_g3, ln_b3)


def _conv_sample_kernel(hist_ref, u_ref, w_ref, cb_ref, lg_ref, lb_ref, c_ref):
    sb, steps, _ = u_ref.shape
    hist = CONV_WIDTH - 1
    accs = [jnp.broadcast_to(cb_ref[...], (sb, D_CONV)) for _ in range(steps)]
    for r in range(hist + steps):
        row = hist_ref[:, r, :] if r < hist else u_ref[:, r - hist, :]
        for t in range(steps):
            if 0 <= r - t < CONV_WIDTH:
                accs[t] = accs[t] + row * w_ref[pl.ds(r - t, 1), :]
    for t in range(steps):
        c_ref[:, t, :] = _ln_swish(accs[t], lg_ref[...], lb_ref[...]).astype(BF16)


def _conv_sample(cache_conv, u_s, conv_w, conv_b3, ln_g3, ln_b3, l, sb):
    db, steps, _ = u_s.shape
    vec = pl.BlockSpec((None, 1, D_CONV), lambda i: (l, 0, 0))
    return pl.pallas_call(
        _conv_sample_kernel,
        grid=(db // sb,),
        in_specs=[
            pl.BlockSpec((None, sb, CONV_WIDTH - 1, D_CONV), lambda i: (l, i, 0, 0)),
            pl.BlockSpec((sb, steps, D_CONV), lambda i: (i, 0, 0)),
            pl.BlockSpec((None, CONV_WIDTH, D_CONV), lambda i: (l, 0, 0)),
            vec, vec, vec,
        ],
        out_specs=pl.BlockSpec((sb, steps, D_CONV), lambda i: (i, 0, 0)),
        out_shape=jax.ShapeDtypeStruct((db, steps, D_CONV), BF16),
        compiler_params=_params("parallel"),
        name="conv_sample",
    )(cache_conv, u_s, conv_w, conv_b3, ln_g3, ln_b3)


def _softmax_sink_pv(s, mask, sink, v_bf, dot_pv):
    s = jnp.where(mask, s, -jnp.inf)
    m = jnp.maximum(jnp.max(s, -1, keepdims=True), sink)
    p = jnp.exp(s - m)
    den = jnp.sum(p, -1, keepdims=True) + jnp.exp(sink - m)
    return dot_pv(p.astype(BF16), v_bf) / den


def _attn_prompt_kernel(sink_ref, q_ref, kp_ref, kc_ref, vp_ref, vc_ref, o_ref, *, l, tq):
    first = pl.program_id(1) == 0
    rows = GROUP * BLOCK
    qi = lax.broadcasted_iota(I32, (rows, 2 * BLOCK), 0) % BLOCK
    kj = lax.broadcasted_iota(I32, (rows, 2 * BLOCK), 1)
    band = (kj > qi) & (kj <= qi + BLOCK)
    head_of_row = lax.broadcasted_iota(I32, (rows, 1), 0) // BLOCK
    dot_qk = lambda a, b: lax.dot_general(a, b, (((1,), (1,)), ((), ())), preferred_element_type=F32)
    dot_pv = lambda a, b: jnp.dot(a, b, preferred_element_type=F32)
    for qb in range(tq // BLOCK):
        if qb == 0:
            kk = jnp.concatenate([kp_ref[...], kc_ref[0:BLOCK, :]], 0)
            vv = jnp.concatenate([vp_ref[...], vc_ref[0:BLOCK, :]], 0)
            mask = band & (kj >= jnp.where(first, BLOCK, 0))
        else:
            kk = kc_ref[(qb - 1) * BLOCK:(qb + 1) * BLOCK, :]
            vv = vc_ref[(qb - 1) * BLOCK:(qb + 1) * BLOCK, :]
            mask = band
        kk = kk.astype(BF16)
        vv = vv.astype(BF16)
        q = q_ref[qb * BLOCK:(qb + 1) * BLOCK, :]
        for g in range(N_KV_HEADS):
            hs = [g * GROUP + i for i in range(GROUP)]
            q4 = jnp.concatenate([q[:, h * HEAD_DIM:(h + 1) * HEAD_DIM] for h in hs], 0)
            sink = jnp.zeros((rows, 1), F32)
            for i, h in enumerate(hs):
                sink = jnp.where(head_of_row == i, sink_ref[l, h], sink)
            s = dot_qk(q4, kk[:, g * HEAD_DIM:(g + 1) * HEAD_DIM])
            o = _softmax_sink_pv(s, mask, sink, vv[:, g * HEAD_DIM:(g + 1) * HEAD_DIM], dot_pv)
            for i, h in enumerate(hs):
                o_ref[qb * BLOCK:(qb + 1) * BLOCK, h * HEAD_DIM:(h + 1) * HEAD_DIM] = (
                    o[i * BLOCK:(i + 1) * BLOCK, :].astype(BF16))


def _attn_prompt(sinks, q_all, k_all, v_all, l, batch, seq, tq):
    nq = seq // tq
    per = tq // BLOCK
    cur = lambda b, j: (b * nq + j, 0)
    prev = lambda b, j: (jnp.maximum((b * nq + j) * per - 1, 0), 0)
    return pl.pallas_call(
        functools.partial(_attn_prompt_kernel, l=l, tq=tq),
        grid=(batch, nq),
        in_specs=[
            pl.BlockSpec(memory_space=pltpu.SMEM),
            pl.BlockSpec((tq, D_ATTN), cur),
            pl.BlockSpec((BLOCK, D_KV), prev),
            pl.BlockSpec((tq, D_KV), cur),
            pl.BlockSpec((BLOCK, D_KV), prev),
            pl.BlockSpec((tq, D_KV), cur),
        ],
        out_specs=pl.BlockSpec((tq, D_ATTN), cur),
        out_shape=jax.ShapeDtypeStruct((batch * seq, D_ATTN), BF16),
        compiler_params=_params("parallel", "parallel"),
        name="attn_prompt",
    )(sinks, q_all, k_all, k_all, v_all, v_all)


def _attn_sample_kernel(sink_ref, q_ref, kn_ref, vn_ref, kc_ref, vc_ref, o_ref, *, l, steps):
    sb = q_ref.shape[0]
    hist = kc_ref.shape[1]
    rows = GROUP * steps
    pad = jnp.zeros((sb, hist - steps, D_KV), F32)
    kk = jnp.concatenate([kc_ref[...], kn_ref[...], pad], 1).astype(BF16)
    vv = jnp.concatenate([vc_ref[...], vn_ref[...], pad], 1).astype(BF16)
    t = lax.broadcasted_iota(I32, (sb, rows, 2 * hist), 1) % steps
    j = lax.broadcasted_iota(I32, (sb, rows, 2 * hist), 2)
    mask = ((j < hist) & (t + hist - j < WINDOW)) | ((j >= hist) & (j - hist <= t) & (t - (j - hist) < WINDOW))
    head_of_row = lax.broadcasted_iota(I32, (rows, 1), 0) // steps
    dot_qk = lambda a, b: jnp.einsum("bqd,bkd->bqk", a, b, preferred_element_type=F32)
    dot_pv = lambda a, b: jnp.einsum("bqk,bkd->bqd", a, b, preferred_element_type=F32)
    q = q_ref[...]
    for g in range(N_KV_HEADS):
        hs = [g * GROUP + i for i in range(GROUP)]
        q4 = jnp.concatenate([q[:, :, h * HEAD_DIM:(h + 1) * HEAD_DIM] for h in hs], 1)
        sink = jnp.zeros((rows, 1), F32)
        for i, h in enumerate(hs):
            sink = jnp.where(head_of_row == i, sink_ref[l, h], sink)
        s = dot_qk(q4, kk[:, :, g * HEAD_DIM:(g + 1) * HEAD_DIM])
        o = _softmax_sink_pv(s, mask, sink[None], vv[:, :, g * HEAD_DIM:(g + 1) * HEAD_DIM], dot_pv)
        for i, h in enumerate(hs):
            o_ref[:, :, h * HEAD_DIM:(h + 1) * HEAD_DIM] = o[:, i * steps:(i + 1) * steps, :].astype(BF16)


def _attn_sample(sinks, q_s, k_new, v_new, cache_k4, cache_v4, l, sb):
    db, steps, _ = q_s.shape
    hist = cache_k4.shape[2]
    seq3 = lambda i: (i, 0, 0)
    return pl.pallas_call(
        functools.partial(_attn_sample_kernel, l=l, steps=steps),
        grid=(db // sb,),
        in_specs=[
            pl.BlockSpec(memory_space=pltpu.SMEM),
            pl.BlockSpec((sb, steps, D_ATTN), seq3),
            pl.BlockSpec((sb, steps, D_KV), seq3),
            pl.BlockSpec((sb, steps, D_KV), seq3),
            pl.BlockSpec((None, sb, hist, D_KV), lambda i: (l, i, 0, 0)),
            pl.BlockSpec((None, sb, hist, D_KV), lambda i: (l, i, 0, 0)),
        ],
        out_specs=pl.BlockSpec((sb, steps, D_ATTN), seq3),
        out_shape=jax.ShapeDtypeStruct((db, steps, D_ATTN), BF16),
        compiler_params=_params("parallel"),
        name="attn_sample",
    )(sinks, q_s, k_new, v_new, cache_k4, cache_v4)


def _out_proj_kernel(c_ref, a_ref, w_ref, x_ref, o_ref):
    o_ref[...] = (x_ref[...]
                  + jnp.dot(c_ref[...], w_ref[0:D_CONV, :], preferred_element_type=F32)
                  + jnp.dot(a_ref[...], w_ref[D_CONV:, :], preferred_element_type=F32))


def _out_proj(c, a, w_out_bf, x_all, l, row0, tile):
    rows = c.shape[0]
    off = row0 // tile
    return pl.pallas_call(
        _out_proj_kernel,
        grid=(rows // tile,),
        in_specs=[
            pl.BlockSpec((tile, D_CONV), lambda i: (i, 0)),
            pl.BlockSpec((tile, D_ATTN), lambda i: (i, 0)),
            pl.BlockSpec((None, D_MODEL, D_MODEL), lambda i: (l, 0, 0)),
            pl.BlockSpec((tile, D_MODEL), lambda i: (off + i, 0)),
        ],
        out_specs=pl.BlockSpec((tile, D_MODEL), lambda i: (off + i, 0)),
        out_shape=jax.ShapeDtypeStruct(x_all.shape, F32),
        input_output_aliases={3: 0},
        compiler_params=_params("parallel"),
        name="out_proj",
    )(c, a, w_out_bf, x_all)


FF_CHUNK = 256


def _swiglu(x, g_ref, wgu_ref, wd_ref, act_ref):
    h = _rms(x, g_ref[...]).astype(BF16)
    for c0 in range(0, D_FF, FF_CHUNK):
        gate = jnp.dot(h, wgu_ref[:, c0:c0 + FF_CHUNK], preferred_element_type=F32)
        up = jnp.dot(h, wgu_ref[:, D_FF + c0:D_FF + c0 + FF_CHUNK], preferred_element_type=F32)
        act_ref[:, c0:c0 + FF_CHUNK] = (gate * _sigmoid(gate) * up).astype(BF16)
    return jnp.dot(act_ref[...], wd_ref[...], preferred_element_type=F32)


def _ffn_dense_kernel(x_ref, g_ref, wgu_ref, wd_ref, o_ref, act_ref):
    x = x_ref[...]
    o_ref[...] = x + _swiglu(x, g_ref, wgu_ref, wd_ref, act_ref)


def _ffn_dense(x_all, g_ffn3, wgu_bf, wd_bf, l, tile):
    n = x_all.shape[0]
    once = pl.Buffered(1)
    return pl.pallas_call(
        _ffn_dense_kernel,
        grid=(n // tile,),
        in_specs=[
            pl.BlockSpec((tile, D_MODEL), lambda i: (i, 0)),
            pl.BlockSpec((None, 1, D_MODEL), lambda i: (l, 0, 0)),
            pl.BlockSpec((None, D_MODEL, 2 * D_FF), lambda i: (l // 2, 0, 0), pipeline_mode=once),
            pl.BlockSpec((None, D_FF, D_MODEL), lambda i: (l // 2, 0, 0), pipeline_mode=once),
        ],
        out_specs=pl.BlockSpec((tile, D_MODEL), lambda i: (i, 0)),
        out_shape=jax.ShapeDtypeStruct(x_all.shape, F32),
        scratch_shapes=[pltpu.VMEM((tile, D_FF), BF16)],
        input_output_aliases={0: 0},
        compiler_params=_params("parallel"),
        name="ffn_dense",
    )(x_all, g_ffn3, wgu_bf, wd_bf)


INFO_E, INFO_G, INFO_R = 0, 2, 4


def _route_kernel(x_ref, g_ref, wh_ref, wl_ref, info_ref, cnt_ref, run_ref):
    @pl.when(pl.program_id(0) == 0)
    def _():
        run_ref[...] = jnp.zeros_like(run_ref)

    h = _rms(x_ref[...], g_ref[...])
    h_hi = h.astype(BF16)
    h_lo = (h - h_hi.astype(F32)).astype(BF16)
    dot = lambda a, b: jnp.dot(a, b, preferred_element_type=F32)
    logits = dot(h_hi, wh_ref[...]) + dot(h_hi, wl_ref[...]) + dot(h_lo, wh_ref[...])
    t = logits.shape[0]
    lane = lax.broadcasted_iota(I32, logits.shape, 1)
    lg = jnp.where(lane < N_EXPERTS, logits, -jnp.inf)
    m1 = jnp.max(lg, -1, keepdims=True)
    i1 = jnp.min(jnp.where(lg == m1, lane, LANES), -1, keepdims=True)
    lg2 = jnp.where(lane == i1, -jnp.inf, lg)
    m2 = jnp.max(lg2, -1, keepdims=True)
    i2 = jnp.min(jnp.where(lg2 == m2, lane, LANES), -1, keepdims=True)
    e = jnp.exp(m2 - m1)
    g1 = 1.0 / (1.0 + e)
    g2 = e / (1.0 + e)
    sel1 = lane == i1
    sel2 = lane == i2
    onehot = jnp.where(sel1 | sel2, 1.0, 0.0)
    r = lax.broadcasted_iota(I32, (t, t), 0)
    c = lax.broadcasted_iota(I32, (t, t), 1)
    tri = jnp.where(r > c, 1.0, 0.0).astype(BF16)
    before = dot(tri, onehot.astype(BF16)) + run_ref[0:1, :]
    r1 = jnp.sum(jnp.where(sel1, before, 0.0), -1, keepdims=True)
    r2 = jnp.sum(jnp.where(sel2, before, 0.0), -1, keepdims=True)
    run_ref[...] = run_ref[...] + jnp.sum(onehot, 0, keepdims=True)
    cnt_ref[...] = run_ref[...]
    info = jnp.zeros(logits.shape, F32)
    for pos, val in ((INFO_E, i1.astype(F32)), (INFO_E + 1, i2.astype(F32)), (INFO_G, g1), (INFO_G + 1, g2),
                     (INFO_R, r1), (INFO_R + 1, r2)):
        info = jnp.where(lane == pos, val, info)
    info_ref[...] = info


def _route(x_all, g_ffn3, wr_hi, wr_lo, l, tile):
    n = x_all.shape[0]
    const = lambda i: (l // 2, 0, 0)
    return pl.pallas_call(
        _route_kernel,
        grid=(n // tile,),
        in_specs=[
            pl.BlockSpec((tile, D_MODEL), lambda i: (i, 0)),
            pl.BlockSpec((None, 1, D_MODEL), lambda i: (l, 0, 0)),
            pl.BlockSpec((None, D_MODEL, LANES), const),
            pl.BlockSpec((None, D_MODEL, LANES), const),
        ],
        out_specs=[
            pl.BlockSpec((tile, LANES), lambda i: (i, 0)),
            pl.BlockSpec((SUBLANES, LANES), lambda i: (0, 0)),
        ],
        out_shape=[
            jax.ShapeDtypeStruct((n, LANES), F32),
            jax.ShapeDtypeStruct((SUBLANES, LANES), F32),
        ],
        scratch_shapes=[pltpu.VMEM((SUBLANES, LANES), F32)],
        compiler_params=_params("arbitrary"),
        name="route",
    )(x_all, g_ffn3, wr_hi, wr_lo)


def _row_copy(src_ref, src_row, dst_ref, dst_row, sem):
    return pltpu.make_async_copy(src_ref.at[pl.ds(src_row, 1)], dst_ref.at[pl.ds(dst_row, 1)], sem)


def _start_row_copies(tile, make_copy):
    def body(pair, carry):
        for parity in range(2):
            for j in range(2):
                make_copy(2 * pair + parity, j).start(priority=parity)
        return carry

    lax.fori_loop(0, tile // 2, body, 0)


def _dispatch_kernel(pad_start_ref, pad_len_ref, pos_ref, x_ref, xs_ref, zero_ref, sem, zsem, *, tile, tm):
    first = pl.program_id(0) == 0

    def pad_copies(action):
        for e in range(N_EXPERTS):
            start = pad_start_ref[e]
            lead = pad_len_ref[e] & (SUBLANES - 1)
            for r in range(SUBLANES - 1):
                @pl.when(r < lead)
                def _():
                    action(_row_copy(zero_ref, 0, xs_ref, start + r, zsem))

            start = pl.multiple_of(start + lead, SUBLANES)
            run = tm // 2
            while run >= SUBLANES:
                take = (pad_len_ref[e] & run) != 0

                @pl.when(take)
                def _():
                    action(pltpu.make_async_copy(zero_ref.at[pl.ds(0, run)], xs_ref.at[pl.ds(start, run)], zsem))

                start = pl.multiple_of(start + jnp.where(take, run, 0), SUBLANES)
                run //= 2
        n_tiles = xs_ref.shape[0] // tm
        for t in range(n_tiles - (N_EXPERTS - 1), n_tiles):
            @pl.when(t >= pad_len_ref[N_EXPERTS])
            def _():
                for half in range(2):
                    action(pltpu.make_async_copy(zero_ref, xs_ref.at[pl.ds(t * tm + half * (tm // 2), tm // 2)], zsem))

    @pl.when(first)
    def _():
        zero_ref[...] = jnp.zeros_like(zero_ref)
        pad_copies(lambda c: c.start())

    _start_row_copies(tile, lambda i, j: _row_copy(x_ref, i, xs_ref, pos_ref[0, j, i], sem.at[j]))
    for j in range(2):
        pltpu.make_async_copy(x_ref, xs_ref.at[pl.ds(0, tile)], sem.at[j]).wait()

    @pl.when(first)
    def _():
        pad_copies(lambda c: c.wait())


def _dispatch(pad_start, pad_len, pos3, x_all, n_rows, tile, tm):
    n = x_all.shape[0]
    grid_spec = pltpu.PrefetchScalarGridSpec(
        num_scalar_prefetch=2,
        grid=(n // tile,),
        in_specs=[
            pl.BlockSpec((1, 2, tile), lambda i, ps, pn: (i, 0, 0), memory_space=pltpu.SMEM),
            pl.BlockSpec((tile, D_MODEL), lambda i, ps, pn: (i, 0)),
        ],
        out_specs=pl.BlockSpec(memory_space=pl.ANY),
        scratch_shapes=[pltpu.VMEM((tm // 2, D_MODEL), F32), pltpu.SemaphoreType.DMA((2,)),
                        pltpu.SemaphoreType.DMA(())],
    )
    return pl.pallas_call(
        functools.partial(_dispatch_kernel, tile=tile, tm=tm),
        grid_spec=grid_spec,
        out_shape=jax.ShapeDtypeStruct((n_rows, D_MODEL), F32),
        compiler_params=_params("arbitrary"),
        name="moe_dispatch",
    )(pad_start, pad_len, pos3, x_all)


def _moe_kernel(te_ref, used_ref, xs_ref, g_ref, wgu_ref, wd_ref, y_ref, act_ref):
    used = pl.program_id(0) < used_ref[0]

    @pl.when(used)
    def _():
        y_ref[...] = _swiglu(xs_ref[...], g_ref, wgu_ref, wd_ref, act_ref)

    @pl.when(jnp.logical_not(used))
    def _():
        y_ref[...] = jnp.zeros_like(y_ref)


def _moe(tile_expert, n_used, xs, g_ffn3, wgu_bf, wd_bf, l, tm):
    nt = xs.shape[0] // tm
    m = l // 2

    def tile_idx(t, used):
        return jnp.minimum(t, used[0] - 1)

    grid_spec = pltpu.PrefetchScalarGridSpec(
        num_scalar_prefetch=2,
        grid=(nt,),
        in_specs=[
            pl.BlockSpec((tm, D_MODEL), lambda t, te, used: (tile_idx(t, used), 0)),
            pl.BlockSpec((None, 1, D_MODEL), lambda t, te, used: (l, 0, 0)),
            pl.BlockSpec((None, None, D_MODEL, 2 * D_FF), lambda t, te, used: (m, te[tile_idx(t, used)], 0, 0)),
            pl.BlockSpec((None, None, D_FF, D_MODEL), lambda t, te, used: (m, te[tile_idx(t, used)], 0, 0)),
        ],
        out_specs=pl.BlockSpec((tm, D_MODEL), lambda t, te, used: (t, 0)),
        scratch_shapes=[pltpu.VMEM((tm, D_FF), BF16)],
    )
    return pl.pallas_call(
        _moe_kernel,
        grid_spec=grid_spec,
        out_shape=jax.ShapeDtypeStruct(xs.shape, F32),
        compiler_params=_params("arbitrary"),
        name="moe_experts",
    )(tile_expert, n_used, xs, g_ffn3, wgu_bf, wd_bf)


def _combine_kernel(pos_ref, info_ref, x_ref, y_ref, o_ref, buf_ref, sem, *, tile):
    _start_row_copies(tile, lambda i, j: _row_copy(y_ref, pos_ref[0, j, i], buf_ref.at[j], i, sem.at[j]))
    for j in range(2):
        pltpu.make_async_copy(y_ref.at[pl.ds(0, tile)], buf_ref.at[j], sem.at[j]).wait()
    info = info_ref[...]
    o_ref[...] = (x_ref[...]
                  + info[:, INFO_G:INFO_G + 1] * buf_ref[0]
                  + info[:, INFO_G + 1:INFO_G + 2] * buf_ref[1])


def _combine(pos3, info, x_all, y, tile):
    n = x_all.shape[0]
    return pl.pallas_call(
        functools.partial(_combine_kernel, tile=tile),
        grid=(n // tile,),
        in_specs=[
            pl.BlockSpec((1, 2, tile), lambda i: (i, 0, 0), memory_space=pltpu.SMEM),
            pl.BlockSpec((tile, LANES), lambda i: (i, 0)),
            pl.BlockSpec((tile, D_MODEL), lambda i: (i, 0)),
            pl.BlockSpec(memory_space=pl.ANY),
        ],
        out_specs=pl.BlockSpec((tile, D_MODEL), lambda i: (i, 0)),
        out_shape=jax.ShapeDtypeStruct(x_all.shape, F32),
        scratch_shapes=[pltpu.VMEM((2, tile, D_MODEL), F32), pltpu.SemaphoreType.DMA((2,))],
        input_output_aliases={2: 0},
        compiler_params=_params("arbitrary"),
        name="moe_combine",
    )(pos3, info, x_all, y)


def _ffn_moe(x_all, g_ffn3, wr_hi, wr_lo, wgu_bf, wd_bf, l, tile, tm):
    n = x_all.shape[0]
    info, cnt = _route(x_all, g_ffn3, wr_hi, wr_lo, l, tile)
    counts = cnt[0, :N_EXPERTS].astype(I32)
    tiles_per = (counts + tm - 1) // tm
    tile_end = jnp.cumsum(tiles_per)
    base = (tile_end - tiles_per) * tm
    experts = info[:, INFO_E:INFO_E + 2].astype(I32)
    ranks = info[:, INFO_R:INFO_R + 2].astype(I32)
    pos = jnp.sum(jnp.where(experts[..., None] == jnp.arange(N_EXPERTS), base, 0), -1) + ranks
    n_tiles = (2 * n + N_EXPERTS * (tm - 1)) // tm
    n_used = tile_end[-1:]
    tile_expert = jnp.minimum(jnp.sum(jnp.arange(n_tiles)[:, None] >= tile_end[None, :], -1), N_EXPERTS - 1)
    pos3 = pos.T.reshape(2, n // tile, tile).transpose(1, 0, 2)
    pad_len = jnp.concatenate([tiles_per * tm - counts, n_used])
    xs = _dispatch(base + counts, pad_len, pos3, x_all, n_tiles * tm, tile, tm)
    y = _moe(tile_expert.astype(I32), n_used.astype(I32), xs, g_ffn3, wgu_bf, wd_bf, l, tm)
    return _combine(pos3, info, x_all, y, tile)


def _final_kernel(x_ref, g_ref, o_ref):
    o_ref[...] = _rms(x_ref[...], g_ref[...])


def _final_norm(x_all, g2, row0, rows, tile):
    off = row0 // tile
    return pl.pallas_call(
        _final_kernel,
        grid=(rows // tile,),
        in_specs=[
            pl.BlockSpec((tile, D_MODEL), lambda i: (off + i, 0)),
            pl.BlockSpec((1, D_MODEL), lambda i: (0, 0)),
        ],
        out_specs=pl.BlockSpec((tile, D_MODEL), lambda i: (i, 0)),
        out_shape=jax.ShapeDtypeStruct((rows, D_MODEL), F32),
        compiler_params=_params("parallel"),
        name="final_norm",
    )(x_all, g2)


def _rope_tables(pos):
    half = HEAD_DIM // 2
    inv = ROPE_THETA ** (-jnp.arange(half, dtype=F32) / half)
    ang = pos.astype(F32)[:, None] * inv[None, :]
    cos = jnp.cos(ang)
    sin = jnp.sin(ang)
    reps = LANES // HEAD_DIM
    return jnp.tile(jnp.concatenate([cos, cos], -1), (1, reps)), jnp.tile(jnp.concatenate([-sin, sin], -1), (1, reps))


def kernel(x_prompt, x_sample, cache_conv, cache_swa_k, cache_swa_v, g_mix, w_in, conv_w, conv_b, conv_ln_g,
           conv_ln_b, attn_sinks, w_out, g_ffn, w_dense_gu, w_dense_down, w_router, w_moe_gu, w_moe_down, g_final):
    batch, seq, _ = x_prompt.shape
    db, steps, _ = x_sample.shape
    depth = g_mix.shape[0]
    hist = cache_swa_k.shape[2]
    n_p, n_s = batch * seq, db * steps
    n = n_p + n_s
    assert seq % BLOCK == 0 and seq >= CONV_WIDTH - 1 and steps <= hist and steps < CONV_WIDTH - 1
    assert hist == min(WINDOW, PAST_LEN)
    tile = _pick_tile((512, 256, 128), n_p, n_s)
    tq = _pick_tile((512, 128), seq)
    tm = _pick_tile((512, 256), 2 * n)
    sb_attn = _pick_tile((16, 8), db)
    sb_conv = _pick_tile((8,), db)

    x_all = jnp.concatenate([x_prompt.reshape(n_p, D_MODEL), x_sample.reshape(n_s, D_MODEL)], 0)
    pos_all = jnp.concatenate([jnp.tile(jnp.arange(seq, dtype=I32), batch),
                               jnp.tile(PAST_LEN + jnp.arange(steps, dtype=I32), db)])
    cos_t, sin_t = _rope_tables(pos_all)

    vec3 = lambda a: a.reshape(a.shape[0], 1, a.shape[1])
    g_mix3, g_ffn3, conv_b3, ln_g3, ln_b3 = map(vec3, (g_mix, g_ffn, conv_b, conv_ln_g, conv_ln_b))
    w_in_bf, w_out_bf = w_in.astype(BF16), w_out.astype(BF16)
    wgu_bf, wd_bf = w_dense_gu.astype(BF16), w_dense_down.astype(BF16)
    moe_gu_bf, moe_d_bf = w_moe_gu.astype(BF16), w_moe_down.astype(BF16)
    wr = jnp.pad(w_router, ((0, 0), (0, 0), (0, LANES - N_EXPERTS)))
    wr_hi = wr.astype(BF16)
    wr_lo = (wr - wr_hi.astype(F32)).astype(BF16)
    cache_k4 = cache_swa_k.reshape(depth, db, hist, D_KV)
    cache_v4 = cache_swa_v.reshape(depth, db, hist, D_KV)

    def tail_rows(a, rows):
        return jnp.stack([a[(b + 1) * seq - rows:(b + 1) * seq] for b in range(batch)], 0)

    keep = min(WINDOW, seq)
    conv_p, k_p, v_p, u_new, k_new, v_new = [], [], [], [], [], []
    for l in range(depth):
        u, q, k, v = _in_proj(x_all, g_mix3, w_in_bf, cos_t, sin_t, l, tile)

        c_p = _conv_prompt(u, conv_w, conv_b3, ln_g3, ln_b3, l, batch, seq, tq)
        a_p = _attn_prompt(attn_sinks, q, k, v, l, batch, seq, tq)
        x_all = _out_proj(c_p, a_p, w_out_bf, x_all, l, 0, tile)

        u_s = u[n_p:].reshape(db, steps, D_CONV)
        k_s = k[n_p:].reshape(db, steps, D_KV)
        v_s = v[n_p:].reshape(db, steps, D_KV)
        c_s = _conv_sample(cache_conv, u_s, conv_w, conv_b3, ln_g3, ln_b3, l, sb_conv).reshape(n_s, D_CONV)
        a_s = _attn_sample(attn_sinks, q[n_p:].reshape(db, steps, D_ATTN), k_s, v_s, cache_k4, cache_v4,
                           l, sb_attn).reshape(n_s, D_ATTN)
        x_all = _out_proj(c_s, a_s, w_out_bf, x_all, l, n_p, tile)

        conv_p.append(tail_rows(u, CONV_WIDTH - 1))
        k_p.append(tail_rows(k, keep).reshape(batch, keep, N_KV_HEADS, HEAD_DIM))
        v_p.append(tail_rows(v, keep).reshape(batch, keep, N_KV_HEADS, HEAD_DIM))
        u_new.append(u_s)
        k_new.append(k_s.reshape(db, steps, N_KV_HEADS, HEAD_DIM))
        v_new.append(v_s.reshape(db, steps, N_KV_HEADS, HEAD_DIM))

        if l % 2 == 0:
            x_all = _ffn_dense(x_all, g_ffn3, wgu_bf, wd_bf, l, tile)
        else:
            x_all = _ffn_moe(x_all, g_ffn3, wr_hi, wr_lo, moe_gu_bf, moe_d_bf, l, tile, tm)

    g_fin = g_final.reshape(1, D_MODEL)
    y_prompt = _final_norm(x_all, g_fin, 0, n_p, tile).reshape(batch, seq, D_MODEL)
    y_sample = _final_norm(x_all, g_fin, n_p, n_s, tile).reshape(db, steps, D_MODEL)
    state_s = lambda cache, new: jnp.concatenate([cache[:, :, steps:], jnp.stack(new, 0)], 2)
    return (y_prompt, y_sample, jnp.stack(conv_p, 0), jnp.stack(k_p, 0), jnp.stack(v_p, 0),
            state_s(cache_conv, u_new), state_s(cache_swa_k, k_new), state_s(cache_swa_v, v_new))
```

```python
import functools

import jax
import jax.numpy as jnp
from jax import lax
from jax.experimental import pallas as pl
from jax.experimental.pallas import tpu as pltpu

F32 = jnp.float32
BF16 = jnp.bfloat16
I32 = jnp.int32

D_MODEL = 1024
D_CONV = 512
N_HEADS = 8
HEAD_DIM = 64
N_KV_HEADS = 2
GROUP = N_HEADS // N_KV_HEADS
D_ATTN = N_HEADS * HEAD_DIM
D_KV = N_KV_HEADS * HEAD_DIM
D_IN = 2 * D_CONV + D_ATTN + 2 * D_KV
CONV_WIDTH = 31
WINDOW = 128
BLOCK = 128
ROPE_THETA = 10000.0
D_FF = 2816
N_EXPERTS = 8
EPS = 1e-6
PAST_LEN = 8192

LANES = 128
SUBLANES = 8
CONV_HALO = 32
CONV_PAD = CONV_HALO - (CONV_WIDTH - 1)
VMEM_LIMIT = 56 * 1024 * 1024


def _pick_tile(cands, *sizes):
    for c in cands:
        if all(s % c == 0 for s in sizes):
            return c
    raise ValueError(f"no tile in {cands} divides {sizes}")


def _params(*sem):
    return pltpu.CompilerParams(dimension_semantics=sem, vmem_limit_bytes=VMEM_LIMIT)


def _rms(x, g):
    return x * lax.rsqrt(jnp.mean(x * x, -1, keepdims=True) + EPS) * g


def _sigmoid(x):
    return 1.0 / (1.0 + jnp.exp(-x))


def _in_proj_kernel(x_ref, g_ref, w_ref, cos_ref, sin_ref, u_ref, q_ref, k_ref, v_ref):
    h = _rms(x_ref[...], g_ref[...]).astype(BF16)
    p = jnp.dot(h, w_ref[...], preferred_element_type=F32)
    u_ref[...] = p[:, :D_CONV] * _sigmoid(p[:, D_CONV:2 * D_CONV])
    cos = cos_ref[...]
    sin = sin_ref[...]
    lane = lax.broadcasted_iota(I32, cos.shape, 1)
    first_half = (lane % HEAD_DIM) < (HEAD_DIM // 2)

    def rope(xc):
        partner = jnp.where(first_half,
                            pltpu.roll(xc, LANES - HEAD_DIM // 2, 1),
                            pltpu.roll(xc, HEAD_DIM // 2, 1))
        return xc * cos + partner * sin

    q0 = 2 * D_CONV
    for c in range(D_ATTN // LANES):
        qc = rope(p[:, q0 + c * LANES:q0 + (c + 1) * LANES])
        q_ref[:, c * LANES:(c + 1) * LANES] = (qc * (HEAD_DIM ** -0.5)).astype(BF16)
    k0 = q0 + D_ATTN
    k_ref[...] = rope(p[:, k0:k0 + D_KV])
    v_ref[...] = p[:, k0 + D_KV:k0 + 2 * D_KV]


def _in_proj(x_all, g_mix3, w_in_bf, cos_t, sin_t, l, tile):
    n = x_all.shape[0]
    row = lambda i: (i, 0)
    return pl.pallas_call(
        _in_proj_kernel,
        grid=(n // tile,),
        in_specs=[
            pl.BlockSpec((tile, D_MODEL), row),
            pl.BlockSpec((None, 1, D_MODEL), lambda i: (l, 0, 0)),
            pl.BlockSpec((None, D_MODEL, D_IN), lambda i: (l, 0, 0)),
            pl.BlockSpec((tile, LANES), row),
            pl.BlockSpec((tile, LANES), row),
        ],
        out_specs=[
            pl.BlockSpec((tile, D_CONV), row),
            pl.BlockSpec((tile, D_ATTN), row),
            pl.BlockSpec((tile, D_KV), row),
            pl.BlockSpec((tile, D_KV), row),
        ],
        out_shape=[
            jax.ShapeDtypeStruct((n, D_CONV), F32),
            jax.ShapeDtypeStruct((n, D_ATTN), BF16),
            jax.ShapeDtypeStruct((n, D_KV), F32),
            jax.ShapeDtypeStruct((n, D_KV), F32),
        ],
        compiler_params=_params("parallel"),
        name="in_proj",
    )(x_all, g_mix3, w_in_bf, cos_t, sin_t)


def _ln_swish(y, g, b):
    mu = jnp.mean(y, -1, keepdims=True)
    yc = y - mu
    z = yc * lax.rsqrt(jnp.mean(yc * yc, -1, keepdims=True) + EPS) * g + b
    return z * _sigmoid(z)


def _conv_prompt_kernel(u_ref, w_ref, cb_ref, lg_ref, lb_ref, c_ref, ext_ref, sh_ref, y_ref, *, tile, rb):
    @pl.when(pl.program_id(1) == 0)
    def _():
        ext_ref[0:CONV_HALO, :] = jnp.zeros((CONV_HALO, D_CONV), F32)

    ext_ref[CONV_HALO:CONV_HALO + tile, :] = u_ref[...]
    for b in range(1, SUBLANES):
        sh_ref[b - 1] = ext_ref[pl.ds(b, sh_ref.shape[1]), :]
    def row_block(i, carry):
        r0 = pl.multiple_of(i * rb, rb)
        acc = jnp.broadcast_to(cb_ref[...], (rb, D_CONV))
        for k in range(CONV_WIDTH):
            b = (CONV_PAD + k) % SUBLANES
            row = pl.multiple_of(r0 + (CONV_PAD + k - b), SUBLANES)
            src = ext_ref[pl.ds(row, rb), :] if b == 0 else sh_ref[b - 1, pl.ds(row, rb), :]
            acc = acc + src * w_ref[pl.ds(k, 1), :]
        y_ref[pl.ds(r0, rb), :] = acc
        return carry

    lax.fori_loop(0, tile // rb, row_block, 0)
    c_ref[...] = _ln_swish(y_ref[...], lg_ref[...], lb_ref[...]).astype(BF16)
    ext_ref[0:CONV_HALO, :] = ext_ref[tile:tile + CONV_HALO, :]


def _conv_prompt(u_all, conv_w, conv_b3, ln_g3, ln_b3, l, batch, seq, tile):
    nt = seq // tile
    vec = pl.BlockSpec((None, 1, D_CONV), lambda b, j: (l, 0, 0))
    return pl.pallas_call(
        functools.partial(_conv_prompt_kernel, tile=tile, rb=32),
        grid=(batch, nt),
        in_specs=[
            pl.BlockSpec((tile, D_CONV), lambda b, j: (b * nt + j, 0)),
            pl.BlockSpec((None, CONV_WIDTH, D_CONV), lambda b, j: (l, 0, 0)),
            vec, vec, vec,
        ],
        out_specs=pl.BlockSpec((tile, D_CONV), lambda b, j: (b * nt + j, 0)),
        out_shape=jax.ShapeDtypeStruct((batch * seq, D_CONV), BF16),
        scratch_shapes=[pltpu.VMEM((tile + CONV_HALO, D_CONV), F32),
                        pltpu.VMEM((SUBLANES - 1, tile + CONV_HALO - SUBLANES, D_CONV), F32),
                        pltpu.VMEM((tile, D_CONV), F32)],
        compiler_params=_params("arbitrary", "arbitrary"),
        name="conv_prompt",
    )(u_all, conv_w, conv_b3, ln_g3, ln_b3)


def _conv_sample_kernel(hist_ref, u_ref, w_ref, cb_ref, lg_ref, lb_ref, c_ref):
    sb, steps, _ = u_ref.shape
    hist = CONV_WIDTH - 1
    accs = [jnp.broadcast_to(cb_ref[...], (sb, D_CONV)) for _ in range(steps)]
    for r in range(hist + steps):
        row = hist_ref[:, r, :] if r < hist else u_ref[:, r - hist, :]
        for t in range(steps):
            if 0 <= r - t < CONV_WIDTH:
                accs[t] = accs[t] + row * w_ref[pl.ds(r - t, 1), :]
    for t in range(steps):
        c_ref[:, t, :] = _ln_swish(accs[t], lg_ref[...], lb_ref[...]).astype(BF16)


def _conv_sample(cache_conv, u_s, conv_w, conv_b3, ln_g3, ln_b3, l, sb):
    db, steps, _ = u_s.shape
    vec = pl.BlockSpec((None, 1, D_CONV), lambda i: (l, 0, 0))
    return pl.pallas_call(
        _conv_sample_kernel,
        grid=(db // sb,),
        in_specs=[
            pl.BlockSpec((None, sb, CONV_WIDTH - 1, D_CONV), lambda i: (l, i, 0, 0)),
            pl.BlockSpec((sb, steps, D_CONV), lambda i: (i, 0, 0)),
            pl.BlockSpec((None, CONV_WIDTH, D_CONV), lambda i: (l, 0, 0)),
            vec, vec, vec,
        ],
        out_specs=pl.BlockSpec((sb, steps, D_CONV), lambda i: (i, 0, 0)),
        out_shape=jax.ShapeDtypeStruct((db, steps, D_CONV), BF16),
        compiler_params=_params("parallel"),
        name="conv_sample",
    )(cache_conv, u_s, conv_w, conv_b3, ln_g3, ln_b3)


def _softmax_sink_pv(s, mask, sink, v_bf, dot_pv):
    s = jnp.where(mask, s, -jnp.inf)
    m = jnp.maximum(jnp.max(s, -1, keepdims=True), sink)
    p = jnp.exp(s - m)
    den = jnp.sum(p, -1, keepdims=True) + jnp.exp(sink - m)
    return dot_pv(p.astype(BF16), v_bf) / den


def _attn_prompt_kernel(sink_ref, q_ref, kp_ref, kc_ref, vp_ref, vc_ref, o_ref, *, l, tq):
    first = pl.program_id(1) == 0
    cols = GROUP * BLOCK
    kj = lax.broadcasted_iota(I32, (2 * BLOCK, cols), 0)
    qi = lax.broadcasted_iota(I32, (2 * BLOCK, cols), 1) % BLOCK
    band = (kj > qi) & (kj <= qi + BLOCK)
    head_of_col = lax.broadcasted_iota(I32, (1, cols), 1) // BLOCK
    for qb in range(tq // BLOCK):
        if qb == 0:
            kk = jnp.concatenate([kp_ref[...], kc_ref[0:BLOCK, :]], 0)
            vv = jnp.concatenate([vp_ref[...], vc_ref[0:BLOCK, :]], 0)
            mask = band & (kj >= jnp.where(first, BLOCK, 0))
        else:
            kk = kc_ref[(qb - 1) * BLOCK:(qb + 1) * BLOCK, :]
            vv = vc_ref[(qb - 1) * BLOCK:(qb + 1) * BLOCK, :]
            mask = band
        kk = kk.astype(BF16)
        q = q_ref[qb * BLOCK:(qb + 1) * BLOCK, :]
        for g in range(N_KV_HEADS):
            hs = [g * GROUP + i for i in range(GROUP)]
            q4 = jnp.concatenate([q[:, h * HEAD_DIM:(h + 1) * HEAD_DIM] for h in hs], 0)
            sink = jnp.zeros((1, cols), F32)
            for i, h in enumerate(hs):
                sink = jnp.where(head_of_col == i, sink_ref[l, h], sink)
            s = lax.dot_general(kk[:, g * HEAD_DIM:(g + 1) * HEAD_DIM], q4, (((1,), (1,)), ((), ())),
                                preferred_element_type=F32)
            s = jnp.where(mask, s, -jnp.inf)
            m = jnp.maximum(jnp.max(s, 0, keepdims=True), sink)
            p = jnp.exp(s - m)
            den = jnp.sum(p, 0, keepdims=True) + jnp.exp(sink - m)
            v_t = vv[:, g * HEAD_DIM:(g + 1) * HEAD_DIM].T.astype(BF16)
            o_t = jnp.dot(v_t, p.astype(BF16), preferred_element_type=F32) * (1.0 / den)
            for i, h in enumerate(hs):
                o_ref[qb * BLOCK:(qb + 1) * BLOCK, h * HEAD_DIM:(h + 1) * HEAD_DIM] = (
                    o_t[:, i * BLOCK:(i + 1) * BLOCK].T.astype(BF16))


def _attn_prompt(sinks, q_all, k_all, v_all, l, batch, seq, tq):
    nq = seq // tq
    per = tq // BLOCK
    cur = lambda b, j: (b * nq + j, 0)
    prev = lambda b, j: (jnp.maximum((b * nq + j) * per - 1, 0), 0)
    return pl.pallas_call(
        functools.partial(_attn_prompt_kernel, l=l, tq=tq),
        grid=(batch, nq),
        in_specs=[
            pl.BlockSpec(memory_space=pltpu.SMEM),
            pl.BlockSpec((tq, D_ATTN), cur),
            pl.BlockSpec((BLOCK, D_KV), prev),
            pl.BlockSpec((tq, D_KV), cur),
            pl.BlockSpec((BLOCK, D_KV), prev),
            pl.BlockSpec((tq, D_KV), cur),
        ],
        out_specs=pl.BlockSpec((tq, D_ATTN), cur),
        out_shape=jax.ShapeDtypeStruct((batch * seq, D_ATTN), BF16),
        compiler_params=_params("parallel", "parallel"),
        name="attn_prompt",
    )(sinks, q_all, k_all, k_all, v_all, v_all)


def _attn_sample_kernel(sink_ref, q_ref, kn_ref, vn_ref, kc_ref, vc_ref, o_ref, *, l, steps):
    sb = q_ref.shape[0]
    hist = kc_ref.shape[1]
    rows = GROUP * steps
    pad = jnp.zeros((sb, hist - steps, D_KV), F32)
    kk = jnp.concatenate([kc_ref[...], kn_ref[...], pad], 1).astype(BF16)
    vv = jnp.concatenate([vc_ref[...], vn_ref[...], pad], 1).astype(BF16)
    t = lax.broadcasted_iota(I32, (sb, rows, 2 * hist), 1) % steps
    j = lax.broadcasted_iota(I32, (sb, rows, 2 * hist), 2)
    mask = ((j < hist) & (t + hist - j < WINDOW)) | ((j >= hist) & (j - hist <= t) & (t - (j - hist) < WINDOW))
    head_of_row = lax.broadcasted_iota(I32, (rows, 1), 0) // steps
    dot_qk = lambda a, b: jnp.einsum("bqd,bkd->bqk", a, b, preferred_element_type=F32)
    dot_pv = lambda a, b: jnp.einsum("bqk,bkd->bqd", a, b, preferred_element_type=F32)
    q = q_ref[...]
    for g in range(N_KV_HEADS):
        hs = [g * GROUP + i for i in range(GROUP)]
        q4 = jnp.concatenate([q[:, :, h * HEAD_DIM:(h + 1) * HEAD_DIM] for h in hs], 1)
        sink = jnp.zeros((rows, 1), F32)
        for i, h in enumerate(hs):
            sink = jnp.where(head_of_row == i, sink_ref[l, h], sink)
        s = dot_qk(q4, kk[:, :, g * HEAD_DIM:(g + 1) * HEAD_DIM])
        o = _softmax_sink_pv(s, mask, sink[None], vv[:, :, g * HEAD_DIM:(g + 1) * HEAD_DIM], dot_pv)
        for i, h in enumerate(hs):
            o_ref[:, :, h * HEAD_DIM:(h + 1) * HEAD_DIM] = o[:, i * steps:(i + 1) * steps, :].astype(BF16)


def _attn_sample(sinks, q_s, k_new, v_new, cache_k4, cache_v4, l, sb):
    db, steps, _ = q_s.shape
    hist = cache_k4.shape[2]
    seq3 = lambda i: (i, 0, 0)
    return pl.pallas_call(
        functools.partial(_attn_sample_kernel, l=l, steps=steps),
        grid=(db // sb,),
        in_specs=[
            pl.BlockSpec(memory_space=pltpu.SMEM),
            pl.BlockSpec((sb, steps, D_ATTN), seq3),
            pl.BlockSpec((sb, steps, D_KV), seq3),
            pl.BlockSpec((sb, steps, D_KV), seq3),
            pl.BlockSpec((None, sb, hist, D_KV), lambda i: (l, i, 0, 0)),
            pl.BlockSpec((None, sb, hist, D_KV), lambda i: (l, i, 0, 0)),
        ],
        out_specs=pl.BlockSpec((sb, steps, D_ATTN), seq3),
        out_shape=jax.ShapeDtypeStruct((db, steps, D_ATTN), BF16),
        compiler_params=_params("parallel"),
        name="attn_sample",
    )(sinks, q_s, k_new, v_new, cache_k4, cache_v4)


def _out_proj_kernel(c_ref, a_ref, w_ref, x_ref, o_ref):
    o_ref[...] = (x_ref[...]
                  + jnp.dot(c_ref[...], w_ref[0:D_CONV, :], preferred_element_type=F32)
                  + jnp.dot(a_ref[...], w_ref[D_CONV:, :], preferred_element_type=F32))


def _out_proj(c, a, w_out_bf, x_all, l, row0, tile):
    rows = c.shape[0]
    off = row0 // tile
    return pl.pallas_call(
        _out_proj_kernel,
        grid=(rows // tile,),
        in_specs=[
            pl.BlockSpec((tile, D_CONV), lambda i: (i, 0)),
            pl.BlockSpec((tile, D_ATTN), lambda i: (i, 0)),
            pl.BlockSpec((None, D_MODEL, D_MODEL), lambda i: (l, 0, 0)),
            pl.BlockSpec((tile, D_MODEL), lambda i: (off + i, 0)),
        ],
        out_specs=pl.BlockSpec((tile, D_MODEL), lambda i: (off + i, 0)),
        out_shape=jax.ShapeDtypeStruct(x_all.shape, F32),
        input_output_aliases={3: 0},
        compiler_params=_params("parallel"),
        name="out_proj",
    )(c, a, w_out_bf, x_all)


FF_CHUNK = 256


def _swiglu(x, g_ref, wgu_ref, wd_ref, act_ref):
    h = _rms(x, g_ref[...]).astype(BF16)
    for c0 in range(0, D_FF, FF_CHUNK):
        gate = jnp.dot(h, wgu_ref[:, c0:c0 + FF_CHUNK], preferred_element_type=F32)
        up = jnp.dot(h, wgu_ref[:, D_FF + c0:D_FF + c0 + FF_CHUNK], preferred_element_type=F32)
        act_ref[:, c0:c0 + FF_CHUNK] = (gate * _sigmoid(gate) * up).astype(BF16)
    return jnp.dot(act_ref[...], wd_ref[...], preferred_element_type=F32)


def _ffn_dense_kernel(x_ref, g_ref, wgu_ref, wd_ref, o_ref, act_ref):
    x = x_ref[...]
    o_ref[...] = x + _swiglu(x, g_ref, wgu_ref, wd_ref, act_ref)


def _ffn_dense(x_all, g_ffn3, wgu_bf, wd_bf, l, tile):
    n = x_all.shape[0]
    once = pl.Buffered(1)
    return pl.pallas_call(
        _ffn_dense_kernel,
        grid=(n // tile,),
        in_specs=[
            pl.BlockSpec((tile, D_MODEL), lambda i: (i, 0)),
            pl.BlockSpec((None, 1, D_MODEL), lambda i: (l, 0, 0)),
            pl.BlockSpec((None, D_MODEL, 2 * D_FF), lambda i: (l // 2, 0, 0), pipeline_mode=once),
            pl.BlockSpec((None, D_FF, D_MODEL), lambda i: (l // 2, 0, 0), pipeline_mode=once),
        ],
        out_specs=pl.BlockSpec((tile, D_MODEL), lambda i: (i, 0)),
        out_shape=jax.ShapeDtypeStruct(x_all.shape, F32),
        scratch_shapes=[pltpu.VMEM((tile, D_FF), BF16)],
        input_output_aliases={0: 0},
        compiler_params=_params("parallel"),
        name="ffn_dense",
    )(x_all, g_ffn3, wgu_bf, wd_bf)


INFO_E, INFO_G, INFO_R = 0, 2, 4


def _route_kernel(x_ref, g_ref, wh_ref, wl_ref, info_ref, cnt_ref, run_ref):
    @pl.when(pl.program_id(0) == 0)
    def _():
        run_ref[...] = jnp.zeros_like(run_ref)

    h = _rms(x_ref[...], g_ref[...])
    h_hi = h.astype(BF16)
    h_lo = (h - h_hi.astype(F32)).astype(BF16)
    dot = lambda a, b: jnp.dot(a, b, preferred_element_type=F32)
    logits = dot(h_hi, wh_ref[...]) + dot(h_hi, wl_ref[...]) + dot(h_lo, wh_ref[...])
    t = logits.shape[0]
    lane = lax.broadcasted_iota(I32, logits.shape, 1)
    lg = jnp.where(lane < N_EXPERTS, logits, -jnp.inf)
    m1 = jnp.max(lg, -1, keepdims=True)
    i1 = jnp.min(jnp.where(lg == m1, lane, LANES), -1, keepdims=True)
    lg2 = jnp.where(lane == i1, -jnp.inf, lg)
    m2 = jnp.max(lg2, -1, keepdims=True)
    i2 = jnp.min(jnp.where(lg2 == m2, lane, LANES), -1, keepdims=True)
    e = jnp.exp(m2 - m1)
    g1 = 1.0 / (1.0 + e)
    g2 = e / (1.0 + e)
    sel1 = lane == i1
    sel2 = lane == i2
    onehot = jnp.where(sel1 | sel2, 1.0, 0.0)
    r = lax.broadcasted_iota(I32, (t, t), 0)
    c = lax.broadcasted_iota(I32, (t, t), 1)
    tri = jnp.where(r > c, 1.0, 0.0).astype(BF16)
    before = dot(tri, onehot.astype(BF16)) + run_ref[0:1, :]
    r1 = jnp.sum(jnp.where(sel1, before, 0.0), -1, keepdims=True)
    r2 = jnp.sum(jnp.where(sel2, before, 0.0), -1, keepdims=True)
    run_ref[...] = run_ref[...] + jnp.sum(onehot, 0, keepdims=True)
    cnt_ref[...] = run_ref[...]
    info = jnp.zeros(logits.shape, F32)
    for pos, val in ((INFO_E, i1.astype(F32)), (INFO_E + 1, i2.astype(F32)), (INFO_G, g1), (INFO_G + 1, g2),
                     (INFO_R, r1), (INFO_R + 1, r2)):
        info = jnp.where(lane == pos, val, info)
    info_ref[...] = info


def _route(x_all, g_ffn3, wr_hi, wr_lo, l, tile):
    n = x_all.shape[0]
    const = lambda i: (l // 2, 0, 0)
    return pl.pallas_call(
        _route_kernel,
        grid=(n // tile,),
        in_specs=[
            pl.BlockSpec((tile, D_MODEL), lambda i: (i, 0)),
            pl.BlockSpec((None, 1, D_MODEL), lambda i: (l, 0, 0)),
            pl.BlockSpec((None, D_MODEL, LANES), const),
            pl.BlockSpec((None, D_MODEL, LANES), const),
        ],
        out_specs=[
            pl.BlockSpec((tile, LANES), lambda i: (i, 0)),
            pl.BlockSpec((SUBLANES, LANES), lambda i: (0, 0)),
        ],
        out_shape=[
            jax.ShapeDtypeStruct((n, LANES), F32),
            jax.ShapeDtypeStruct((SUBLANES, LANES), F32),
        ],
        scratch_shapes=[pltpu.VMEM((SUBLANES, LANES), F32)],
        compiler_params=_params("arbitrary"),
        name="route",
    )(x_all, g_ffn3, wr_hi, wr_lo)


def _row_copy(src_ref, src_row, dst_ref, dst_row, sem):
    return pltpu.make_async_copy(src_ref.at[pl.ds(src_row, 1)], dst_ref.at[pl.ds(dst_row, 1)], sem)


def _start_row_copies(tile, make_copy):
    def body(pair, carry):
        for parity in range(2):
            for j in range(2):
                make_copy(2 * pair + parity, j).start(priority=parity)
        return carry

    lax.fori_loop(0, tile // 2, body, 0)


def _dispatch_kernel(pad_start_ref, pad_len_ref, pos_ref, x_ref, xs_ref, zero_ref, sem, zsem, *, tile, tm):
    first = pl.program_id(0) == 0

    def pad_copies(action):
        for e in range(N_EXPERTS):
            start = pad_start_ref[e]
            lead = pad_len_ref[e] & (SUBLANES - 1)
            for r in range(SUBLANES - 1):
                @pl.when(r < lead)
                def _():
                    action(_row_copy(zero_ref, 0, xs_ref, start + r, zsem))

            start = pl.multiple_of(start + lead, SUBLANES)
            run = tm // 2
            while run >= SUBLANES:
                take = (pad_len_ref[e] & run) != 0

                @pl.when(take)
                def _():
                    action(pltpu.make_async_copy(zero_ref.at[pl.ds(0, run)], xs_ref.at[pl.ds(start, run)], zsem))

                start = pl.multiple_of(start + jnp.where(take, run, 0), SUBLANES)
                run //= 2
        n_tiles = xs_ref.shape[0] // tm
        for t in range(n_tiles - (N_EXPERTS - 1), n_tiles):
            @pl.when(t >= pad_len_ref[N_EXPERTS])
            def _():
                for half in range(2):
                    action(pltpu.make_async_copy(zero_ref, xs_ref.at[pl.ds(t * tm + half * (tm // 2), tm // 2)], zsem))

    @pl.when(first)
    def _():
        zero_ref[...] = jnp.zeros_like(zero_ref)
        pad_copies(lambda c: c.start())

    _start_row_copies(tile, lambda i, j: _row_copy(x_ref, i, xs_ref, pos_ref[0, j, i], sem.at[j]))
    for j in range(2):
        pltpu.make_async_copy(x_ref, xs_ref.at[pl.ds(0, tile)], sem.at[j]).wait()

    @pl.when(first)
    def _():
        pad_copies(lambda c: c.wait())


def _dispatch(pad_start, pad_len, pos3, x_all, n_rows, tile, tm):
    n = x_all.shape[0]
    grid_spec = pltpu.PrefetchScalarGridSpec(
        num_scalar_prefetch=2,
        grid=(n // tile,),
        in_specs=[
            pl.BlockSpec((1, 2, tile), lambda i, ps, pn: (i, 0, 0), memory_space=pltpu.SMEM),
            pl.BlockSpec((tile, D_MODEL), lambda i, ps, pn: (i, 0)),
        ],
        out_specs=pl.BlockSpec(memory_space=pl.ANY),
        scratch_shapes=[pltpu.VMEM((tm // 2, D_MODEL), F32), pltpu.SemaphoreType.DMA((2,)),
                        pltpu.SemaphoreType.DMA(())],
    )
    return pl.pallas_call(
        functools.partial(_dispatch_kernel, tile=tile, tm=tm),
        grid_spec=grid_spec,
        out_shape=jax.ShapeDtypeStruct((n_rows, D_MODEL), F32),
        compiler_params=_params("arbitrary"),
        name="moe_dispatch",
    )(pad_start, pad_len, pos3, x_all)


def _moe_kernel(te_ref, used_ref, xs_ref, g_ref, wgu_ref, wd_ref, y_ref, act_ref):
    used = pl.program_id(0) < used_ref[0]

    @pl.when(used)
    def _():
        y_ref[...] = _swiglu(xs_ref[...], g_ref, wgu_ref, wd_ref, act_ref)

    @pl.when(jnp.logical_not(used))
    def _():
        y_ref[...] = jnp.zeros_like(y_ref)


def _moe(tile_expert, n_used, xs, g_ffn3, wgu_bf, wd_bf, l, tm):
    nt = xs.shape[0] // tm
    m = l // 2

    def tile_idx(t, used):
        return jnp.minimum(t, used[0] - 1)

    grid_spec = pltpu.PrefetchScalarGridSpec(
        num_scalar_prefetch=2,
        grid=(nt,),
        in_specs=[
            pl.BlockSpec((tm, D_MODEL), lambda t, te, used: (tile_idx(t, used), 0)),
            pl.BlockSpec((None, 1, D_MODEL), lambda t, te, used: (l, 0, 0)),
            pl.BlockSpec((None, None, D_MODEL, 2 * D_FF), lambda t, te, used: (m, te[tile_idx(t, used)], 0, 0)),
            pl.BlockSpec((None, None, D_FF, D_MODEL), lambda t, te, used: (m, te[tile_idx(t, used)], 0, 0)),
        ],
        out_specs=pl.BlockSpec((tm, D_MODEL), lambda t, te, used: (t, 0)),
        scratch_shapes=[pltpu.VMEM((tm, D_FF), BF16)],
    )
    return pl.pallas_call(
        _moe_kernel,
        grid_spec=grid_spec,
        out_shape=jax.ShapeDtypeStruct(xs.shape, F32),
        compiler_params=_params("arbitrary"),
        name="moe_experts",
    )(tile_expert, n_used, xs, g_ffn3, wgu_bf, wd_bf)


def _combine_kernel(pos_ref, info_ref, x_ref, y_ref, o_ref, buf_ref, sem, *, tile):
    _start_row_copies(tile, lambda i, j: _row_copy(y_ref, pos_ref[0, j, i], buf_ref.at[j], i, sem.at[j]))
    for j in range(2):
        pltpu.make_async_copy(y_ref.at[pl.ds(0, tile)], buf_ref.at[j], sem.at[j]).wait()
    info = info_ref[...]
    o_ref[...] = (x_ref[...]
                  + info[:, INFO_G:INFO_G + 1] * buf_ref[0]
                  + info[:, INFO_G + 1:INFO_G + 2] * buf_ref[1])


def _combine(pos3, info, x_all, y, tile):
    n = x_all.shape[0]
    return pl.pallas_call(
        functools.partial(_combine_kernel, tile=tile),
        grid=(n // tile,),
        in_specs=[
            pl.BlockSpec((1, 2, tile), lambda i: (i, 0, 0), memory_space=pltpu.SMEM),
            pl.BlockSpec((tile, LANES), lambda i: (i, 0)),
            pl.BlockSpec((tile, D_MODEL), lambda i: (i, 0)),
            pl.BlockSpec(memory_space=pl.ANY),
        ],
        out_specs=pl.BlockSpec((tile, D_MODEL), lambda i: (i, 0)),
        out_shape=jax.ShapeDtypeStruct(x_all.shape, F32),
        scratch_shapes=[pltpu.VMEM((2, tile, D_MODEL), F32), pltpu.SemaphoreType.DMA((2,))],
        input_output_aliases={2: 0},
        compiler_params=_params("arbitrary"),
        name="moe_combine",
    )(pos3, info, x_all, y)


def _ffn_moe(x_all, g_ffn3, wr_hi, wr_lo, wgu_bf, wd_bf, l, tile, tm):
    n = x_all.shape[0]
    info, cnt = _route(x_all, g_ffn3, wr_hi, wr_lo, l, tile)
    counts = cnt[0, :N_EXPERTS].astype(I32)
    tiles_per = (counts + tm - 1) // tm
    tile_end = jnp.cumsum(tiles_per)
    base = (tile_end - tiles_per) * tm
    experts = info[:, INFO_E:INFO_E + 2].astype(I32)
    ranks = info[:, INFO_R:INFO_R + 2].astype(I32)
    pos = jnp.sum(jnp.where(experts[..., None] == jnp.arange(N_EXPERTS), base, 0), -1) + ranks
    n_tiles = (2 * n + N_EXPERTS * (tm - 1)) // tm
    n_used = tile_end[-1:]
    tile_expert = jnp.minimum(jnp.sum(jnp.arange(n_tiles)[:, None] >= tile_end[None, :], -1), N_EXPERTS - 1)
    pos3 = pos.T.reshape(2, n // tile, tile).transpose(1, 0, 2)
    pad_len = jnp.concatenate([tiles_per * tm - counts, n_used])
    xs = _dispatch(base + counts, pad_len, pos3, x_all, n_tiles * tm, tile, tm)
    y = _moe(tile_expert.astype(I32), n_used.astype(I32), xs, g_ffn3, wgu_bf, wd_bf, l, tm)
    return _combine(pos3, info, x_all, y, tile)


def _final_kernel(x_ref, g_ref, o_ref):
    o_ref[...] = _rms(x_ref[...], g_ref[...])


def _final_norm(x_all, g2, row0, rows, tile):
    off = row0 // tile
    return pl.pallas_call(
        _final_kernel,
        grid=(rows // tile,),
        in_specs=[
            pl.BlockSpec((tile, D_MODEL), lambda i: (off + i, 0)),
            pl.BlockSpec((1, D_MODEL), lambda i: (0, 0)),
        ],
        out_specs=pl.BlockSpec((tile, D_MODEL), lambda i: (i, 0)),
        out_shape=jax.ShapeDtypeStruct((rows, D_MODEL), F32),
        compiler_params=_params("parallel"),
        name="final_norm",
    )(x_all, g2)


def _rope_tables(pos):
    half = HEAD_DIM // 2
    inv = ROPE_THETA ** (-jnp.arange(half, dtype=F32) / half)
    ang = pos.astype(F32)[:, None] * inv[None, :]
    cos = jnp.cos(ang)
    sin = jnp.sin(ang)
    reps = LANES // HEAD_DIM
    return jnp.tile(jnp.concatenate([cos, cos], -1), (1, reps)), jnp.tile(jnp.concatenate([-sin, sin], -1), (1, reps))


def kernel(x_prompt, x_sample, cache_conv, cache_swa_k, cache_swa_v, g_mix, w_in, conv_w, conv_b, conv_ln_g,
           conv_ln_b, attn_sinks, w_out, g_ffn, w_dense_gu, w_dense_down, w_router, w_moe_gu, w_moe_down, g_final):
    batch, seq, _ = x_prompt.shape
    db, steps, _ = x_sample.shape
    depth = g_mix.shape[0]
    hist = cache_swa_k.shape[2]
    n_p, n_s = batch * seq, db * steps
    n = n_p + n_s
    assert seq % BLOCK == 0 and seq >= CONV_WIDTH - 1 and steps <= hist and steps < CONV_WIDTH - 1
    assert hist == min(WINDOW, PAST_LEN)
    tile = _pick_tile((1024, 512, 256, 128), n_p, n_s)
    tq = _pick_tile((512, 128), seq)
    tm = _pick_tile((512, 256), 2 * n)
    sb_attn = _pick_tile((16, 8), db)
    sb_conv = _pick_tile((8,), db)

    x_all = jnp.concatenate([x_prompt.reshape(n_p, D_MODEL), x_sample.reshape(n_s, D_MODEL)], 0)
    pos_all = jnp.concatenate([jnp.tile(jnp.arange(seq, dtype=I32), batch),
                               jnp.tile(PAST_LEN + jnp.arange(steps, dtype=I32), db)])
    cos_t, sin_t = _rope_tables(pos_all)

    vec3 = lambda a: a.reshape(a.shape[0], 1, a.shape[1])
    g_mix3, g_ffn3, conv_b3, ln_g3, ln_b3 = map(vec3, (g_mix, g_ffn, conv_b, conv_ln_g, conv_ln_b))
    w_in_bf, w_out_bf = w_in.astype(BF16), w_out.astype(BF16)
    wgu_bf, wd_bf = w_dense_gu.astype(BF16), w_dense_down.astype(BF16)
    moe_gu_bf, moe_d_bf = w_moe_gu.astype(BF16), w_moe_down.astype(BF16)
    wr = jnp.pad(w_router, ((0, 0), (0, 0), (0, LANES - N_EXPERTS)))
    wr_hi = wr.astype(BF16)
    wr_lo = (wr - wr_hi.astype(F32)).astype(BF16)
    cache_k4 = cache_swa_k.reshape(depth, db, hist, D_KV)
    cache_v4 = cache_swa_v.reshape(depth, db, hist, D_KV)

    def tail_rows(a, rows):
        return jnp.stack([a[(b + 1) * seq - rows:(b + 1) * seq] for b in range(batch)], 0)

    keep = min(WINDOW, seq)
    conv_p, k_p, v_p, u_new, k_new, v_new = [], [], [], [], [], []
    for l in range(depth):
        u, q, k, v = _in_proj(x_all, g_mix3, w_in_bf, cos_t, sin_t, l, tile)

        c_p = _conv_prompt(u, conv_w, conv_b3, ln_g3, ln_b3, l, batch, seq, tq)
        a_p = _attn_prompt(attn_sinks, q, k, v, l, batch, seq, tq)
        x_all = _out_proj(c_p, a_p, w_out_bf, x_all, l, 0, tile)

        u_s = u[n_p:].reshape(db, steps, D_CONV)
        k_s = k[n_p:].reshape(db, steps, D_KV)
        v_s = v[n_p:].reshape(db, steps, D_KV)
        c_s = _conv_sample(cache_conv, u_s, conv_w, conv_b3, ln_g3, ln_b3, l, sb_conv).reshape(n_s, D_CONV)
        a_s = _attn_sample(attn_sinks, q[n_p:].reshape(db, steps, D_ATTN), k_s, v_s, cache_k4, cache_v4,
                           l, sb_attn).reshape(n_s, D_ATTN)
        x_all = _out_proj(c_s, a_s, w_out_bf, x_all, l, n_p, tile)

        conv_p.append(tail_rows(u, CONV_WIDTH - 1))
        k_p.append(tail_rows(k, keep).reshape(batch, keep, N_KV_HEADS, HEAD_DIM))
        v_p.append(tail_rows(v, keep).reshape(batch, keep, N_KV_HEADS, HEAD_DIM))
        u_new.append(u_s)
        k_new.append(k_s.reshape(db, steps, N_KV_HEADS, HEAD_DIM))
        v_new.append(v_s.reshape(db, steps, N_KV_HEADS, HEAD_DIM))

        if l % 2 == 0:
            x_all = _ffn_dense(x_all, g_ffn3, wgu_bf, wd_bf, l, tile)
        else:
            x_all = _ffn_moe(x_all, g_ffn3, wr_hi, wr_lo, moe_gu_bf, moe_d_bf, l, tile, tm)

    g_fin = g_final.reshape(1, D_MODEL)
    y_prompt = _final_norm(x_all, g_fin, 0, n_p, tile).reshape(batch, seq, D_MODEL)
    y_sample = _final_norm(x_all, g_fin, n_p, n_s, tile).reshape(db, steps, D_MODEL)
    state_s = lambda cache, new: jnp.concatenate([cache[:, :, steps:], jnp.stack(new, 0)], 2)
    return (y_prompt, y_sample, jnp.stack(conv_p, 0), jnp.stack(k_p, 0), jnp.stack(v_p, 0),
            state_s(cache_conv, u_new), state_s(cache_swa_k, k_new), state_s(cache_swa_v, v_new))
```

```python
import functools

import jax
import jax.numpy as jnp
from jax import lax
from jax.experimental import pallas as pl
from jax.experimental.pallas import tpu as pltpu

F32 = jnp.float32
BF16 = jnp.bfloat16
I32 = jnp.int32

D_MODEL = 1024
D_CONV = 512
N_HEADS = 8
HEAD_DIM = 64
N_KV_HEADS = 2
GROUP = N_HEADS // N_KV_HEADS
D_ATTN = N_HEADS * HEAD_DIM
D_KV = N_KV_HEADS * HEAD_DIM
D_IN = 2 * D_CONV + D_ATTN + 2 * D_KV
CONV_WIDTH = 31
WINDOW = 128
BLOCK = 128
ROPE_THETA = 10000.0
D_FF = 2816
N_EXPERTS = 8
EPS = 1e-6
PAST_LEN = 8192

LANES = 128
SUBLANES = 8
CONV_HALO = 32
CONV_PAD = CONV_HALO - (CONV_WIDTH - 1)
VMEM_LIMIT = 56 * 1024 * 1024


def _pick_tile(cands, *sizes):
    for c in cands:
        if all(s % c == 0 for s in sizes):
            return c
    raise ValueError(f"no tile in {cands} divides {sizes}")


def _params(*sem):
    return pltpu.CompilerParams(dimension_semantics=sem, vmem_limit_bytes=VMEM_LIMIT)


def _rms(x, g):
    return x * lax.rsqrt(jnp.mean(x * x, -1, keepdims=True) + EPS) * g


def _sigmoid(x):
    return 1.0 / (1.0 + jnp.exp(-x))


def _in_proj_kernel(x_ref, g_ref, w_ref, cos_ref, sin_ref, u_ref, q_ref, k_ref, v_ref):
    h = _rms(x_ref[...], g_ref[...]).astype(BF16)
    p = jnp.dot(h, w_ref[...], preferred_element_type=F32)
    u_ref[...] = p[:, :D_CONV] * _sigmoid(p[:, D_CONV:2 * D_CONV])
    cos = cos_ref[...]
    sin = sin_ref[...]
    lane = lax.broadcasted_iota(I32, cos.shape, 1)
    first_half = (lane % HEAD_DIM) < (HEAD_DIM // 2)

    def rope(xc):
        partner = jnp.where(first_half,
                            pltpu.roll(xc, LANES - HEAD_DIM // 2, 1),
                            pltpu.roll(xc, HEAD_DIM // 2, 1))
        return xc * cos + partner * sin

    q0 = 2 * D_CONV
    for c in range(D_ATTN // LANES):
        qc = rope(p[:, q0 + c * LANES:q0 + (c + 1) * LANES])
        q_ref[:, c * LANES:(c + 1) * LANES] = (qc * (HEAD_DIM ** -0.5)).astype(BF16)
    k0 = q0 + D_ATTN
    k_ref[...] = rope(p[:, k0:k0 + D_KV])
    v_ref[...] = p[:, k0 + D_KV:k0 + 2 * D_KV]


def _in_proj(x_all, g_mix3, w_in_bf, cos_t, sin_t, l, tile):
    n = x_all.shape[0]
    row = lambda i: (i, 0)
    return pl.pallas_call(
        _in_proj_kernel,
        grid=(n // tile,),
        in_specs=[
            pl.BlockSpec((tile, D_MODEL), row),
            pl.BlockSpec((None, 1, D_MODEL), lambda i: (l, 0, 0)),
            pl.BlockSpec((None, D_MODEL, D_IN), lambda i: (l, 0, 0)),
            pl.BlockSpec((tile, LANES), row),
            pl.BlockSpec((tile, LANES), row),
        ],
        out_specs=[
            pl.BlockSpec((tile, D_CONV), row),
            pl.BlockSpec((tile, D_ATTN), row),
            pl.BlockSpec((tile, D_KV), row),
            pl.BlockSpec((tile, D_KV), row),
        ],
        out_shape=[
            jax.ShapeDtypeStruct((n, D_CONV), F32),
            jax.ShapeDtypeStruct((n, D_ATTN), BF16),
            jax.ShapeDtypeStruct((n, D_KV), F32),
            jax.ShapeDtypeStruct((n, D_KV), F32),
        ],
        compiler_params=_params("parallel"),
        name="in_proj",
    )(x_all, g_mix3, w_in_bf, cos_t, sin_t)


def _ln_swish(y, g, b):
    mu = jnp.mean(y, -1, keepdims=True)
    yc = y - mu
    z = yc * lax.rsqrt(jnp.mean(yc * yc, -1, keepdims=True) + EPS) * g + b
    return z * _sigmoid(z)


def _conv_prompt_kernel(u_ref, w_ref, cb_ref, lg_ref, lb_ref, c_ref, ext_ref, sh_ref, y_ref, *, tile, rb):
    @pl.when(pl.program_id(1) == 0)
    def _():
        ext_ref[0:CONV_HALO, :] = jnp.zeros((CONV_HALO, D_CONV), F32)

    ext_ref[CONV_HALO:CONV_HALO + tile, :] = u_ref[...]
    for b in range(1, SUBLANES):
        sh_ref[b - 1] = ext_ref[pl.ds(b, sh_ref.shape[1]), :]
    def row_block(i, carry):
        r0 = pl.multiple_of(i * rb, rb)
        acc = jnp.broadcast_to(cb_ref[...], (rb, D_CONV))
        for k in range(CONV_WIDTH):
            b = (CONV_PAD + k) % SUBLANES
            row = pl.multiple_of(r0 + (CONV_PAD + k - b), SUBLANES)
            src = ext_ref[pl.ds(row, rb), :] if b == 0 else sh_ref[b - 1, pl.ds(row, rb), :]
            acc = acc + src * w_ref[pl.ds(k, 1), :]
        y_ref[pl.ds(r0, rb), :] = acc
        return carry

    lax.fori_loop(0, tile // rb, row_block, 0)
    c_ref[...] = _ln_swish(y_ref[...], lg_ref[...], lb_ref[...]).astype(BF16)
    ext_ref[0:CONV_HALO, :] = ext_ref[tile:tile + CONV_HALO, :]


def _conv_prompt(u_all, conv_w, conv_b3, ln_g3, ln_b3, l, batch, seq, tile):
    nt = seq // tile
    vec = pl.BlockSpec((None, 1, D_CONV), lambda b, j: (l, 0, 0))
    return pl.pallas_call(
        functools.partial(_conv_prompt_kernel, tile=tile, rb=32),
        grid=(batch, nt),
        in_specs=[
            pl.BlockSpec((tile, D_CONV), lambda b, j: (b * nt + j, 0)),
            pl.BlockSpec((None, CONV_WIDTH, D_CONV), lambda b, j: (l, 0, 0)),
            vec, vec, vec,
        ],
        out_specs=pl.BlockSpec((tile, D_CONV), lambda b, j: (b * nt + j, 0)),
        out_shape=jax.ShapeDtypeStruct((batch * seq, D_CONV), BF16),
        scratch_shapes=[pltpu.VMEM((tile + CONV_HALO, D_CONV), F32),
                        pltpu.VMEM((SUBLANES - 1, tile + CONV_HALO - SUBLANES, D_CONV), F32),
                        pltpu.VMEM((tile, D_CONV), F32)],
        compiler_params=_params("arbitrary", "arbitrary"),
        name="conv_prompt",
    )(u_all, conv_w, conv_b3, ln_g3, ln_b3)


def _conv_sample_kernel(hist_ref, u_ref, w_ref, cb_ref, lg_ref, lb_ref, c_ref):
    sb, steps, _ = u_ref.shape
    hist = CONV_WIDTH - 1
    accs = [jnp.broadcast_to(cb_ref[...], (sb, D_CONV)) for _ in range(steps)]
    for r in range(hist + steps):
        row = hist_ref[:, r, :] if r < hist else u_ref[:, r - hist, :]
        for t in range(steps):
            if 0 <= r - t < CONV_WIDTH:
                accs[t] = accs[t] + row * w_ref[pl.ds(r - t, 1), :]
    for t in range(steps):
        c_ref[:, t, :] = _ln_swish(accs[t], lg_ref[...], lb_ref[...]).astype(BF16)


def _conv_sample(cache_conv, u_s, conv_w, conv_b3, ln_g3, ln_b3, l, sb):
    db, steps, _ = u_s.shape
    vec = pl.BlockSpec((None, 1, D_CONV), lambda i: (l, 0, 0))
    return pl.pallas_call(
        _conv_sample_kernel,
        grid=(db // sb,),
        in_specs=[
            pl.BlockSpec((None, sb, CONV_WIDTH - 1, D_CONV), lambda i: (l, i, 0, 0)),
            pl.BlockSpec((sb, steps, D_CONV), lambda i: (i, 0, 0)),
            pl.BlockSpec((None, CONV_WIDTH, D_CONV), lambda i: (l, 0, 0)),
            vec, vec, vec,
        ],
        out_specs=pl.BlockSpec((sb, steps, D_CONV), lambda i: (i, 0, 0)),
        out_shape=jax.ShapeDtypeStruct((db, steps, D_CONV), BF16),
        compiler_params=_params("parallel"),
        name="conv_sample",
    )(cache_conv, u_s, conv_w, conv_b3, ln_g3, ln_b3)


def _softmax_sink_pv(s, mask, sink, v_bf, dot_pv):
    s = jnp.where(mask, s, -jnp.inf)
    m = jnp.maximum(jnp.max(s, -1, keepdims=True), sink)
    p = jnp.exp(s - m)
    den = jnp.sum(p, -1, keepdims=True) + jnp.exp(sink - m)
    return dot_pv(p.astype(BF16), v_bf) / den


def _attn_prompt_kernel(sink_ref, q_ref, kp_ref, kc_ref, vp_ref, vc_ref, o_ref, *, l, tq):
    first = pl.program_id(1) == 0
    cols = GROUP * BLOCK
    kj = lax.broadcasted_iota(I32, (2 * BLOCK, cols), 0)
    qi = lax.broadcasted_iota(I32, (2 * BLOCK, cols), 1) % BLOCK
    band = (kj > qi) & (kj <= qi + BLOCK)
    head_of_col = lax.broadcasted_iota(I32, (1, cols), 1) // BLOCK
    for qb in range(tq // BLOCK):
        if qb == 0:
            kk = jnp.concatenate([kp_ref[...], kc_ref[0:BLOCK, :]], 0)
            vv = jnp.concatenate([vp_ref[...], vc_ref[0:BLOCK, :]], 0)
            mask = band & (kj >= jnp.where(first, BLOCK, 0))
        else:
            kk = kc_ref[(qb - 1) * BLOCK:(qb + 1) * BLOCK, :]
            vv = vc_ref[(qb - 1) * BLOCK:(qb + 1) * BLOCK, :]
            mask = band
        kk = kk.astype(BF16)
        q = q_ref[qb * BLOCK:(qb + 1) * BLOCK, :]
        for g in range(N_KV_HEADS):
            hs = [g * GROUP + i for i in range(GROUP)]
            q4 = jnp.concatenate([q[:, h * HEAD_DIM:(h + 1) * HEAD_DIM] for h in hs], 0)
            sink = jnp.zeros((1, cols), F32)
            for i, h in enumerate(hs):
                sink = jnp.where(head_of_col == i, sink_ref[l, h], sink)
            s = lax.dot_general(kk[:, g * HEAD_DIM:(g + 1) * HEAD_DIM], q4, (((1,), (1,)), ((), ())),
                                preferred_element_type=F32)
            s = jnp.where(mask, s, -jnp.inf)
            m = jnp.maximum(jnp.max(s, 0, keepdims=True), sink)
            p = jnp.exp(s - m)
            den = jnp.sum(p, 0, keepdims=True) + jnp.exp(sink - m)
            v_t = vv[:, g * HEAD_DIM:(g + 1) * HEAD_DIM].T.astype(BF16)
            o_t = jnp.dot(v_t, p.astype(BF16), preferred_element_type=F32) * (1.0 / den)
            for i, h in enumerate(hs):
                o_ref[qb * BLOCK:(qb + 1) * BLOCK, h * HEAD_DIM:(h + 1) * HEAD_DIM] = (
                    o_t[:, i * BLOCK:(i + 1) * BLOCK].T.astype(BF16))


def _attn_prompt(sinks, q_all, k_all, v_all, l, batch, seq, tq):
    nq = seq // tq
    per = tq // BLOCK
    cur = lambda b, j: (b * nq + j, 0)
    prev = lambda b, j: (jnp.maximum((b * nq + j) * per - 1, 0), 0)
    return pl.pallas_call(
        functools.partial(_attn_prompt_kernel, l=l, tq=tq),
        grid=(batch, nq),
        in_specs=[
            pl.BlockSpec(memory_space=pltpu.SMEM),
            pl.BlockSpec((tq, D_ATTN), cur),
            pl.BlockSpec((BLOCK, D_KV), prev),
            pl.BlockSpec((tq, D_KV), cur),
            pl.BlockSpec((BLOCK, D_KV), prev),
            pl.BlockSpec((tq, D_KV), cur),
        ],
        out_specs=pl.BlockSpec((tq, D_ATTN), cur),
        out_shape=jax.ShapeDtypeStruct((batch * seq, D_ATTN), BF16),
        compiler_params=_params("parallel", "parallel"),
        name="attn_prompt",
    )(sinks, q_all, k_all, k_all, v_all, v_all)


def _attn_sample_kernel(sink_ref, q_ref, kn_ref, vn_ref, kc_ref, vc_ref, o_ref, *, l, steps):
    sb = q_ref.shape[0]
    hist = kc_ref.shape[1]
    rows = GROUP * steps
    pad = jnp.zeros((sb, hist - steps, D_KV), F32)
    kk = jnp.concatenate([kc_ref[...], kn_ref[...], pad], 1).astype(BF16)
    vv = jnp.concatenate([vc_ref[...], vn_ref[...], pad], 1).astype(BF16)
    t = lax.broadcasted_iota(I32, (sb, rows, 2 * hist), 1) % steps
    j = lax.broadcasted_iota(I32, (sb, rows, 2 * hist), 2)
    mask = ((j < hist) & (t + hist - j < WINDOW)) | ((j >= hist) & (j - hist <= t) & (t - (j - hist) < WINDOW))
    head_of_row = lax.broadcasted_iota(I32, (rows, 1), 0) // steps
    dot_qk = lambda a, b: jnp.einsum("bqd,bkd->bqk", a, b, preferred_element_type=F32)
    dot_pv = lambda a, b: jnp.einsum("bqk,bkd->bqd", a, b, preferred_element_type=F32)
    q = q_ref[...]
    for g in range(N_KV_HEADS):
        hs = [g * GROUP + i for i in range(GROUP)]
        q4 = jnp.concatenate([q[:, :, h * HEAD_DIM:(h + 1) * HEAD_DIM] for h in hs], 1)
        sink = jnp.zeros((rows, 1), F32)
        for i, h in enumerate(hs):
            sink = jnp.where(head_of_row == i, sink_ref[l, h], sink)
        s = dot_qk(q4, kk[:, :, g * HEAD_DIM:(g + 1) * HEAD_DIM])
        o = _softmax_sink_pv(s, mask, sink[None], vv[:, :, g * HEAD_DIM:(g + 1) * HEAD_DIM], dot_pv)
        for i, h in enumerate(hs):
            o_ref[:, :, h * HEAD_DIM:(h + 1) * HEAD_DIM] = o[:, i * steps:(i + 1) * steps, :].astype(BF16)


def _attn_sample(sinks, q_s, k_new, v_new, cache_k4, cache_v4, l, sb):
    db, steps, _ = q_s.shape
    hist = cache_k4.shape[2]
    seq3 = lambda i: (i, 0, 0)
    return pl.pallas_call(
        functools.partial(_attn_sample_kernel, l=l, steps=steps),
        grid=(db // sb,),
        in_specs=[
            pl.BlockSpec(memory_space=pltpu.SMEM),
            pl.BlockSpec((sb, steps, D_ATTN), seq3),
            pl.BlockSpec((sb, steps, D_KV), seq3),
            pl.BlockSpec((sb, steps, D_KV), seq3),
            pl.BlockSpec((None, sb, hist, D_KV), lambda i: (l, i, 0, 0)),
            pl.BlockSpec((None, sb, hist, D_KV), lambda i: (l, i, 0, 0)),
        ],
        out_specs=pl.BlockSpec((sb, steps, D_ATTN), seq3),
        out_shape=jax.ShapeDtypeStruct((db, steps, D_ATTN), BF16),
        compiler_params=_params("parallel"),
        name="attn_sample",
    )(sinks, q_s, k_new, v_new, cache_k4, cache_v4)


def _out_proj_kernel(c_ref, a_ref, w_ref, x_ref, o_ref):
    o_ref[...] = (x_ref[...]
                  + jnp.dot(c_ref[...], w_ref[0:D_CONV, :], preferred_element_type=F32)
                  + jnp.dot(a_ref[...], w_ref[D_CONV:, :], preferred_element_type=F32))


def _out_proj(c, a, w_out_bf, x_all, l, row0, tile):
    rows = c.shape[0]
    off = row0 // tile
    return pl.pallas_call(
        _out_proj_kernel,
        grid=(rows // tile,),
        in_specs=[
            pl.BlockSpec((tile, D_CONV), lambda i: (i, 0)),
            pl.BlockSpec((tile, D_ATTN), lambda i: (i, 0)),
            pl.BlockSpec((None, D_MODEL, D_MODEL), lambda i: (l, 0, 0)),
            pl.BlockSpec((tile, D_MODEL), lambda i: (off + i, 0)),
        ],
        out_specs=pl.BlockSpec((tile, D_MODEL), lambda i: (off + i, 0)),
        out_shape=jax.ShapeDtypeStruct(x_all.shape, F32),
        input_output_aliases={3: 0},
        compiler_params=_params("parallel"),
        name="out_proj",
    )(c, a, w_out_bf, x_all)


FF_CHUNK = 256


def _swiglu(x, g_ref, wgu_ref, wd_ref, act_ref):
    h = _rms(x, g_ref[...]).astype(BF16)
    for c0 in range(0, D_FF, FF_CHUNK):
        gate = jnp.dot(h, wgu_ref[:, c0:c0 + FF_CHUNK], preferred_element_type=F32)
        up = jnp.dot(h, wgu_ref[:, D_FF + c0:D_FF + c0 + FF_CHUNK], preferred_element_type=F32)
        act_ref[:, c0:c0 + FF_CHUNK] = (gate * _sigmoid(gate) * up).astype(BF16)
    return jnp.dot(act_ref[...], wd_ref[...], preferred_element_type=F32)


def _ffn_dense_kernel(x_ref, g_ref, wgu_ref, wd_ref, o_ref, act_ref):
    x = x_ref[...]
    o_ref[...] = x + _swiglu(x, g_ref, wgu_ref, wd_ref, act_ref)


def _ffn_dense(x_all, g_ffn3, wgu_bf, wd_bf, l, tile):
    n = x_all.shape[0]
    once = pl.Buffered(1)
    return pl.pallas_call(
        _ffn_dense_kernel,
        grid=(n // tile,),
        in_specs=[
            pl.BlockSpec((tile, D_MODEL), lambda i: (i, 0)),
            pl.BlockSpec((None, 1, D_MODEL), lambda i: (l, 0, 0)),
            pl.BlockSpec((None, D_MODEL, 2 * D_FF), lambda i: (l // 2, 0, 0), pipeline_mode=once),
            pl.BlockSpec((None, D_FF, D_MODEL), lambda i: (l // 2, 0, 0), pipeline_mode=once),
        ],
        out_specs=pl.BlockSpec((tile, D_MODEL), lambda i: (i, 0)),
        out_shape=jax.ShapeDtypeStruct(x_all.shape, F32),
        scratch_shapes=[pltpu.VMEM((tile, D_FF), BF16)],
        input_output_aliases={0: 0},
        compiler_params=_params("parallel"),
        name="ffn_dense",
    )(x_all, g_ffn3, wgu_bf, wd_bf)


INFO_E, INFO_G, INFO_R = 0, 2, 4


def _route_kernel(x_ref, g_ref, wh_ref, wl_ref, info_ref, cnt_ref, run_ref):
    @pl.when(pl.program_id(0) == 0)
    def _():
        run_ref[...] = jnp.zeros_like(run_ref)

    h = _rms(x_ref[...], g_ref[...])
    h_hi = h.astype(BF16)
    h_lo = (h - h_hi.astype(F32)).astype(BF16)
    dot = lambda a, b: jnp.dot(a, b, preferred_element_type=F32)
    logits = dot(h_hi, wh_ref[...]) + dot(h_hi, wl_ref[...]) + dot(h_lo, wh_ref[...])
    t = logits.shape[0]
    lane = lax.broadcasted_iota(I32, logits.shape, 1)
    lg = jnp.where(lane < N_EXPERTS, logits, -jnp.inf)
    m1 = jnp.max(lg, -1, keepdims=True)
    i1 = jnp.min(jnp.where(lg == m1, lane, LANES), -1, keepdims=True)
    lg2 = jnp.where(lane == i1, -jnp.inf, lg)
    m2 = jnp.max(lg2, -1, keepdims=True)
    i2 = jnp.min(jnp.where(lg2 == m2, lane, LANES), -1, keepdims=True)
    e = jnp.exp(m2 - m1)
    g1 = 1.0 / (1.0 + e)
    g2 = e / (1.0 + e)
    sel1 = lane == i1
    sel2 = lane == i2
    onehot = jnp.where(sel1 | sel2, 1.0, 0.0)
    r = lax.broadcasted_iota(I32, (t, t), 0)
    c = lax.broadcasted_iota(I32, (t, t), 1)
    tri = jnp.where(r > c, 1.0, 0.0).astype(BF16)
    before = dot(tri, onehot.astype(BF16)) + run_ref[0:1, :]
    r1 = jnp.sum(jnp.where(sel1, before, 0.0), -1, keepdims=True)
    r2 = jnp.sum(jnp.where(sel2, before, 0.0), -1, keepdims=True)
    run_ref[...] = run_ref[...] + jnp.sum(onehot, 0, keepdims=True)
    cnt_ref[...] = run_ref[...]
    info = jnp.zeros(logits.shape, F32)
    for pos, val in ((INFO_E, i1.astype(F32)), (INFO_E + 1, i2.astype(F32)), (INFO_G, g1), (INFO_G + 1, g2),
                     (INFO_R, r1), (INFO_R + 1, r2)):
        info = jnp.where(lane == pos, val, info)
    info_ref[...] = info


def _route(x_all, g_ffn3, wr_hi, wr_lo, l, tile):
    n = x_all.shape[0]
    const = lambda i: (l // 2, 0, 0)
    return pl.pallas_call(
        _route_kernel,
        grid=(n // tile,),
        in_specs=[
            pl.BlockSpec((tile, D_MODEL), lambda i: (i, 0)),
            pl.BlockSpec((None, 1, D_MODEL), lambda i: (l, 0, 0)),
            pl.BlockSpec((None, D_MODEL, LANES), const),
            pl.BlockSpec((None, D_MODEL, LANES), const),
        ],
        out_specs=[
            pl.BlockSpec((tile, LANES), lambda i: (i, 0)),
            pl.BlockSpec((SUBLANES, LANES), lambda i: (0, 0)),
        ],
        out_shape=[
            jax.ShapeDtypeStruct((n, LANES), F32),
            jax.ShapeDtypeStruct((SUBLANES, LANES), F32),
        ],
        scratch_shapes=[pltpu.VMEM((SUBLANES, LANES), F32)],
        compiler_params=_params("arbitrary"),
        name="route",
    )(x_all, g_ffn3, wr_hi, wr_lo)


def _row_copy(src_ref, src_row, dst_ref, dst_row, sem):
    return pltpu.make_async_copy(src_ref.at[pl.ds(src_row, 1)], dst_ref.at[pl.ds(dst_row, 1)], sem)


def _start_row_copies(tile, make_copy):
    def body(group, carry):
        for row in range(SUBLANES):
            for j in range(2):
                make_copy(group, row, j).start(priority=row % 2)
        return carry

    lax.fori_loop(0, tile // SUBLANES, body, 0)


CAST_PARTS = 8


def _dispatch_kernel(pad_start_ref, pad_len_ref, pos_ref, x_ref, wgu_ref, wd_ref, xs_ref, wgu_bf_ref, wd_bf_ref,
                     zero_ref, sem, zsem, *, tile, tm):
    first = pl.program_id(0) == 0
    wgu_bf_ref[...] = wgu_ref[...].astype(BF16)
    wd_bf_ref[...] = wd_ref[...].astype(BF16)

    def pad_copies(action):
        for e in range(N_EXPERTS):
            start = pad_start_ref[e]
            lead = pad_len_ref[e] & (SUBLANES - 1)
            for r in range(SUBLANES - 1):
                @pl.when(r < lead)
                def _():
                    action(_row_copy(zero_ref, 0, xs_ref, start + r, zsem))

            start = pl.multiple_of(start + lead, SUBLANES)
            run = tm // 2
            while run >= SUBLANES:
                take = (pad_len_ref[e] & run) != 0

                @pl.when(take)
                def _():
                    action(pltpu.make_async_copy(zero_ref.at[pl.ds(0, run)], xs_ref.at[pl.ds(start, run)], zsem))

                start = pl.multiple_of(start + jnp.where(take, run, 0), SUBLANES)
                run //= 2
        n_tiles = xs_ref.shape[0] // tm
        for t in range(n_tiles - (N_EXPERTS - 1), n_tiles):
            @pl.when(t >= pad_len_ref[N_EXPERTS])
            def _():
                for half in range(2):
                    action(pltpu.make_async_copy(zero_ref, xs_ref.at[pl.ds(t * tm + half * (tm // 2), tm // 2)], zsem))

    @pl.when(first)
    def _():
        zero_ref[...] = jnp.zeros_like(zero_ref)
        pad_copies(lambda c: c.start())

    _start_row_copies(tile, lambda grp, row, j: pltpu.make_async_copy(
        x_ref.at[grp, pl.ds(row, 1)], xs_ref.at[pl.ds(pos_ref[0, j, grp * SUBLANES + row], 1)], sem.at[j]))
    for j in range(2):
        pltpu.make_async_copy(xs_ref.at[pl.ds(0, tile)], xs_ref.at[pl.ds(0, tile)], sem.at[j]).wait()

    @pl.when(first)
    def _():
        pad_copies(lambda c: c.wait())


def _dispatch(pad_start, pad_len, pos3, x_all, w_gu, w_down, l, n_rows, tile, tm):
    n = x_all.shape[0]
    steps = n // tile
    assert steps >= N_EXPERTS * CAST_PARTS
    m = l // 2
    gu_rows, d_rows = D_MODEL // CAST_PARTS, D_FF // CAST_PARTS

    def part(i):
        c = jnp.minimum(i, N_EXPERTS * CAST_PARTS - 1)
        return c // CAST_PARTS, c % CAST_PARTS

    grid_spec = pltpu.PrefetchScalarGridSpec(
        num_scalar_prefetch=2,
        grid=(steps,),
        in_specs=[
            pl.BlockSpec((1, 2, tile), lambda i, ps, pn: (i, 0, 0), memory_space=pltpu.SMEM),
            pl.BlockSpec((tile // SUBLANES, SUBLANES, D_MODEL), lambda i, ps, pn: (i, 0, 0)),
            pl.BlockSpec((None, None, gu_rows, 2 * D_FF), lambda i, ps, pn: (m, *part(i), 0)),
            pl.BlockSpec((None, None, d_rows, D_MODEL), lambda i, ps, pn: (m, *part(i), 0)),
        ],
        out_specs=[
            pl.BlockSpec(memory_space=pl.ANY),
            pl.BlockSpec((None, gu_rows, 2 * D_FF), lambda i, ps, pn: (*part(i), 0)),
            pl.BlockSpec((None, d_rows, D_MODEL), lambda i, ps, pn: (*part(i), 0)),
        ],
        scratch_shapes=[pltpu.VMEM((tm // 2, D_MODEL), F32), pltpu.SemaphoreType.DMA((2,)),
                        pltpu.SemaphoreType.DMA(())],
    )
    return pl.pallas_call(
        functools.partial(_dispatch_kernel, tile=tile, tm=tm),
        grid_spec=grid_spec,
        out_shape=[
            jax.ShapeDtypeStruct((n_rows, D_MODEL), F32),
            jax.ShapeDtypeStruct((N_EXPERTS, D_MODEL, 2 * D_FF), BF16),
            jax.ShapeDtypeStruct((N_EXPERTS, D_FF, D_MODEL), BF16),
        ],
        compiler_params=_params("arbitrary"),
        name="moe_dispatch",
    )(pad_start, pad_len, pos3, x_all.reshape(n // SUBLANES, SUBLANES, D_MODEL), w_gu, w_down)


def _moe_kernel(te_ref, used_ref, xs_ref, g_ref, wgu_ref, wd_ref, y_ref, act_ref):
    used = pl.program_id(0) < used_ref[0]

    @pl.when(used)
    def _():
        y_ref[...] = _swiglu(xs_ref[...], g_ref, wgu_ref, wd_ref, act_ref)

    @pl.when(jnp.logical_not(used))
    def _():
        y_ref[...] = jnp.zeros_like(y_ref)


def _moe(tile_expert, n_used, xs, g_ffn3, wgu_bf, wd_bf, l, tm):
    nt = xs.shape[0] // tm

    def tile_idx(t, used):
        return jnp.minimum(t, used[0] - 1)

    grid_spec = pltpu.PrefetchScalarGridSpec(
        num_scalar_prefetch=2,
        grid=(nt,),
        in_specs=[
            pl.BlockSpec((tm, D_MODEL), lambda t, te, used: (tile_idx(t, used), 0)),
            pl.BlockSpec((None, 1, D_MODEL), lambda t, te, used: (l, 0, 0)),
            pl.BlockSpec((None, D_MODEL, 2 * D_FF), lambda t, te, used: (te[tile_idx(t, used)], 0, 0)),
            pl.BlockSpec((None, D_FF, D_MODEL), lambda t, te, used: (te[tile_idx(t, used)], 0, 0)),
        ],
        out_specs=pl.BlockSpec((tm, D_MODEL), lambda t, te, used: (t, 0)),
        scratch_shapes=[pltpu.VMEM((tm, D_FF), BF16)],
    )
    return pl.pallas_call(
        _moe_kernel,
        grid_spec=grid_spec,
        out_shape=jax.ShapeDtypeStruct(xs.shape, F32),
        compiler_params=_params("arbitrary"),
        name="moe_experts",
    )(tile_expert, n_used, xs, g_ffn3, wgu_bf, wd_bf)


def _combine_kernel(pos_ref, info_ref, x_ref, y_ref, o_ref, buf_ref, sem, *, tile):
    _start_row_copies(tile, lambda grp, row, j: pltpu.make_async_copy(
        y_ref.at[pl.ds(pos_ref[0, j, grp * SUBLANES + row], 1)], buf_ref.at[j, grp, pl.ds(row, 1)], sem.at[j]))
    for j in range(2):
        pltpu.make_async_copy(y_ref.at[pl.ds(0, tile)], y_ref.at[pl.ds(0, tile)], sem.at[j]).wait()
    info = info_ref[...]
    o_ref[...] = (x_ref[...]
                  + info[:, INFO_G:INFO_G + 1] * buf_ref[0].reshape(tile, D_MODEL)
                  + info[:, INFO_G + 1:INFO_G + 2] * buf_ref[1].reshape(tile, D_MODEL))


def _combine(pos3, info, x_all, y, tile):
    n = x_all.shape[0]
    return pl.pallas_call(
        functools.partial(_combine_kernel, tile=tile),
        grid=(n // tile,),
        in_specs=[
            pl.BlockSpec((1, 2, tile), lambda i: (i, 0, 0), memory_space=pltpu.SMEM),
            pl.BlockSpec((tile, LANES), lambda i: (i, 0)),
            pl.BlockSpec((tile, D_MODEL), lambda i: (i, 0)),
            pl.BlockSpec(memory_space=pl.ANY),
        ],
        out_specs=pl.BlockSpec((tile, D_MODEL), lambda i: (i, 0)),
        out_shape=jax.ShapeDtypeStruct(x_all.shape, F32),
        scratch_shapes=[pltpu.VMEM((2, tile // SUBLANES, SUBLANES, D_MODEL), F32), pltpu.SemaphoreType.DMA((2,))],
        input_output_aliases={2: 0},
        compiler_params=_params("arbitrary"),
        name="moe_combine",
    )(pos3, info, x_all, y)


def _ffn_moe(x_all, g_ffn3, wr_hi, wr_lo, w_moe_gu, w_moe_down, l, tile, tm):
    n = x_all.shape[0]
    td = next(t for t in (256, 128, 64, 32, 16, 8) if n % t == 0 and n // t >= N_EXPERTS * CAST_PARTS)
    info, cnt = _route(x_all, g_ffn3, wr_hi, wr_lo, l, tile)
    counts = cnt[0, :N_EXPERTS].astype(I32)
    tiles_per = (counts + tm - 1) // tm
    tile_end = jnp.cumsum(tiles_per)
    base = (tile_end - tiles_per) * tm
    experts = info[:, INFO_E:INFO_E + 2].astype(I32)
    ranks = info[:, INFO_R:INFO_R + 2].astype(I32)
    pos = jnp.sum(jnp.where(experts[..., None] == jnp.arange(N_EXPERTS), base, 0), -1) + ranks
    n_tiles = (2 * n + N_EXPERTS * (tm - 1)) // tm
    n_used = tile_end[-1:]
    tile_expert = jnp.minimum(jnp.sum(jnp.arange(n_tiles)[:, None] >= tile_end[None, :], -1), N_EXPERTS - 1)
    blocked = lambda t: pos.T.reshape(2, n // t, t).transpose(1, 0, 2)
    pad_len = jnp.concatenate([tiles_per * tm - counts, n_used])
    xs, wgu_bf, wd_bf = _dispatch(base + counts, pad_len, blocked(td), x_all, w_moe_gu, w_moe_down, l,
                                  n_tiles * tm, td, tm)
    y = _moe(tile_expert.astype(I32), n_used.astype(I32), xs, g_ffn3, wgu_bf, wd_bf, l, tm)
    return _combine(blocked(tile), info, x_all, y, tile)


def _final_kernel(x_ref, g_ref, o_ref):
    o_ref[...] = _rms(x_ref[...], g_ref[...])


def _final_norm(x_all, g2, row0, rows, tile):
    off = row0 // tile
    return pl.pallas_call(
        _final_kernel,
        grid=(rows // tile,),
        in_specs=[
            pl.BlockSpec((tile, D_MODEL), lambda i: (off + i, 0)),
            pl.BlockSpec((1, D_MODEL), lambda i: (0, 0)),
        ],
        out_specs=pl.BlockSpec((tile, D_MODEL), lambda i: (i, 0)),
        out_shape=jax.ShapeDtypeStruct((rows, D_MODEL), F32),
        compiler_params=_params("parallel"),
        name="final_norm",
    )(x_all, g2)


def _rope_tables(pos):
    half = HEAD_DIM // 2
    inv = ROPE_THETA ** (-jnp.arange(half, dtype=F32) / half)
    ang = pos.astype(F32)[:, None] * inv[None, :]
    cos = jnp.cos(ang)
    sin = jnp.sin(ang)
    reps = LANES // HEAD_DIM
    return jnp.tile(jnp.concatenate([cos, cos], -1), (1, reps)), jnp.tile(jnp.concatenate([-sin, sin], -1), (1, reps))


def kernel(x_prompt, x_sample, cache_conv, cache_swa_k, cache_swa_v, g_mix, w_in, conv_w, conv_b, conv_ln_g,
           conv_ln_b, attn_sinks, w_out, g_ffn, w_dense_gu, w_dense_down, w_router, w_moe_gu, w_moe_down, g_final):
    batch, seq, _ = x_prompt.shape
    db, steps, _ = x_sample.shape
    depth = g_mix.shape[0]
    hist = cache_swa_k.shape[2]
    n_p, n_s = batch * seq, db * steps
    n = n_p + n_s
    assert seq % BLOCK == 0 and seq >= CONV_WIDTH - 1 and steps <= hist and steps < CONV_WIDTH - 1
    assert hist == min(WINDOW, PAST_LEN)
    tile = _pick_tile((1024, 512, 256, 128), n_p, n_s)
    tq = _pick_tile((512, 128), seq)
    tm = _pick_tile((512, 256), 2 * n)
    sb_attn = _pick_tile((16, 8), db)
    sb_conv = _pick_tile((8,), db)

    x_all = jnp.concatenate([x_prompt.reshape(n_p, D_MODEL), x_sample.reshape(n_s, D_MODEL)], 0)
    pos_all = jnp.concatenate([jnp.tile(jnp.arange(seq, dtype=I32), batch),
                               jnp.tile(PAST_LEN + jnp.arange(steps, dtype=I32), db)])
    cos_t, sin_t = _rope_tables(pos_all)

    vec3 = lambda a: a.reshape(a.shape[0], 1, a.shape[1])
    g_mix3, g_ffn3, conv_b3, ln_g3, ln_b3 = map(vec3, (g_mix, g_ffn, conv_b, conv_ln_g, conv_ln_b))
    w_in_bf, w_out_bf = w_in.astype(BF16), w_out.astype(BF16)
    wgu_bf, wd_bf = w_dense_gu.astype(BF16), w_dense_down.astype(BF16)
    wr = jnp.pad(w_router, ((0, 0), (0, 0), (0, LANES - N_EXPERTS)))
    wr_hi = wr.astype(BF16)
    wr_lo = (wr - wr_hi.astype(F32)).astype(BF16)
    cache_k4 = cache_swa_k.reshape(depth, db, hist, D_KV)
    cache_v4 = cache_swa_v.reshape(depth, db, hist, D_KV)

    def tail_rows(a, rows):
        return jnp.stack([a[(b + 1) * seq - rows:(b + 1) * seq] for b in range(batch)], 0)

    keep = min(WINDOW, seq)
    conv_p, k_p, v_p, u_new, k_new, v_new = [], [], [], [], [], []
    for l in range(depth):
        u, q, k, v = _in_proj(x_all, g_mix3, w_in_bf, cos_t, sin_t, l, tile)

        c_p = _conv_prompt(u, conv_w, conv_b3, ln_g3, ln_b3, l, batch, seq, tq)
        a_p = _attn_prompt(attn_sinks, q, k, v, l, batch, seq, tq)
        x_all = _out_proj(c_p, a_p, w_out_bf, x_all, l, 0, tile)

        u_s = u[n_p:].reshape(db, steps, D_CONV)
        k_s = k[n_p:].reshape(db, steps, D_KV)
        v_s = v[n_p:].reshape(db, steps, D_KV)
        c_s = _conv_sample(cache_conv, u_s, conv_w, conv_b3, ln_g3, ln_b3, l, sb_conv).reshape(n_s, D_CONV)
        a_s = _attn_sample(attn_sinks, q[n_p:].reshape(db, steps, D_ATTN), k_s, v_s, cache_k4, cache_v4,
                           l, sb_attn).reshape(n_s, D_ATTN)
        x_all = _out_proj(c_s, a_s, w_out_bf, x_all, l, n_p, tile)

        conv_p.append(tail_rows(u, CONV_WIDTH - 1))
        k_p.append(tail_rows(k, keep).reshape(batch, keep, N_KV_HEADS, HEAD_DIM))
        v_p.append(tail_rows(v, keep).reshape(batch, keep, N_KV_HEADS, HEAD_DIM))
        u_new.append(u_s)
        k_new.append(k_s.reshape(db, steps, N_KV_HEADS, HEAD_DIM))
        v_new.append(v_s.reshape(db, steps, N_KV_HEADS, HEAD_DIM))

        if l % 2 == 0:
            x_all = _ffn_dense(x_all, g_ffn3, wgu_bf, wd_bf, l, tile)
        else:
            x_all = _ffn_moe(x_all, g_ffn3, wr_hi, wr_lo, w_moe_gu, w_moe_down, l, tile, tm)

    g_fin = g_final.reshape(1, D_MODEL)
    y_prompt = _final_norm(x_all, g_fin, 0, n_p, tile).reshape(batch, seq, D_MODEL)
    y_sample = _final_norm(x_all, g_fin, n_p, n_s, tile).reshape(db, steps, D_MODEL)
    state_s = lambda cache, new: jnp.concatenate([cache[:, :, steps:], jnp.stack(new, 0)], 2)
    return (y_prompt, y_sample, jnp.stack(conv_p, 0), jnp.stack(k_p, 0), jnp.stack(v_p, 0),
            state_s(cache_conv, u_new), state_s(cache_swa_k, k_new), state_s(cache_swa_v, v_new))
```

```python
import functools

import jax
import jax.numpy as jnp
from jax import lax
from jax.experimental import pallas as pl
from jax.experimental.pallas import tpu as pltpu

F32 = jnp.float32
BF16 = jnp.bfloat16
I32 = jnp.int32

D_MODEL = 1024
D_CONV = 512
N_HEADS = 8
HEAD_DIM = 64
N_KV_HEADS = 2
GROUP = N_HEADS // N_KV_HEADS
D_ATTN = N_HEADS * HEAD_DIM
D_KV = N_KV_HEADS * HEAD_DIM
D_IN = 2 * D_CONV + D_ATTN + 2 * D_KV
CONV_WIDTH = 31
WINDOW = 128
BLOCK = 128
ROPE_THETA = 10000.0
D_FF = 2816
N_EXPERTS = 8
EPS = 1e-6
PAST_LEN = 8192

LANES = 128
SUBLANES = 8
CONV_HALO = 32
CONV_PAD = CONV_HALO - (CONV_WIDTH - 1)
VMEM_LIMIT = 56 * 1024 * 1024


def _pick_tile(cands, *sizes):
    for c in cands:
        if all(s % c == 0 for s in sizes):
            return c
    raise ValueError(f"no tile in {cands} divides {sizes}")


def _params(*sem):
    return pltpu.CompilerParams(dimension_semantics=sem, vmem_limit_bytes=VMEM_LIMIT)


def _rms(x, g):
    return x * lax.rsqrt(jnp.mean(x * x, -1, keepdims=True) + EPS) * g


def _sigmoid(x):
    return 1.0 / (1.0 + jnp.exp(-x))


def _in_proj_kernel(x_ref, g_ref, w_ref, cos_ref, sin_ref, u_ref, q_ref, k_ref, v_ref):
    h = _rms(x_ref[...], g_ref[...]).astype(BF16)
    p = jnp.dot(h, w_ref[...], preferred_element_type=F32)
    u_ref[...] = p[:, :D_CONV] * _sigmoid(p[:, D_CONV:2 * D_CONV])
    cos = cos_ref[...]
    sin = sin_ref[...]
    lane = lax.broadcasted_iota(I32, cos.shape, 1)
    first_half = (lane % HEAD_DIM) < (HEAD_DIM // 2)

    def rope(xc):
        partner = jnp.where(first_half,
                            pltpu.roll(xc, LANES - HEAD_DIM // 2, 1),
                            pltpu.roll(xc, HEAD_DIM // 2, 1))
        return xc * cos + partner * sin

    q0 = 2 * D_CONV
    for c in range(D_ATTN // LANES):
        qc = rope(p[:, q0 + c * LANES:q0 + (c + 1) * LANES])
        q_ref[:, c * LANES:(c + 1) * LANES] = (qc * (HEAD_DIM ** -0.5)).astype(BF16)
    k0 = q0 + D_ATTN
    k_ref[...] = rope(p[:, k0:k0 + D_KV])
    v_ref[...] = p[:, k0 + D_KV:k0 + 2 * D_KV]


def _in_proj(x_all, g_mix3, w_in_bf, cos_t, sin_t, l, tile):
    n = x_all.shape[0]
    row = lambda i: (i, 0)
    return pl.pallas_call(
        _in_proj_kernel,
        grid=(n // tile,),
        in_specs=[
            pl.BlockSpec((tile, D_MODEL), row),
            pl.BlockSpec((None, 1, D_MODEL), lambda i: (l, 0, 0)),
            pl.BlockSpec((None, D_MODEL, D_IN), lambda i: (l, 0, 0)),
            pl.BlockSpec((tile, LANES), row),
            pl.BlockSpec((tile, LANES), row),
        ],
        out_specs=[
            pl.BlockSpec((tile, D_CONV), row),
            pl.BlockSpec((tile, D_ATTN), row),
            pl.BlockSpec((tile, D_KV), row),
            pl.BlockSpec((tile, D_KV), row),
        ],
        out_shape=[
            jax.ShapeDtypeStruct((n, D_CONV), F32),
            jax.ShapeDtypeStruct((n, D_ATTN), BF16),
            jax.ShapeDtypeStruct((n, D_KV), F32),
            jax.ShapeDtypeStruct((n, D_KV), F32),
        ],
        compiler_params=_params("parallel"),
        name="in_proj",
    )(x_all, g_mix3, w_in_bf, cos_t, sin_t)


def _ln_swish(y, g, b):
    mu = jnp.mean(y, -1, keepdims=True)
    yc = y - mu
    z = yc * lax.rsqrt(jnp.mean(yc * yc, -1, keepdims=True) + EPS) * g + b
    return z * _sigmoid(z)


def _conv_prompt_kernel(u_ref, wb_ref, cb_ref, lg_ref, lb_ref, c_ref, ext_ref, sh_ref, y_ref, *, tile, rb):
    @pl.when(pl.program_id(1) == 0)
    def _():
        ext_ref[0:CONV_HALO, :] = jnp.zeros((CONV_HALO, D_CONV), F32)

    ext_ref[CONV_HALO:CONV_HALO + tile, :] = u_ref[...]
    for b in range(1, SUBLANES):
        sh_ref[b - 1] = ext_ref[pl.ds(b, sh_ref.shape[1]), :]
    def row_block(i, carry):
        r0 = pl.multiple_of(i * rb, rb)
        accs = [jnp.broadcast_to(cb_ref[...], (SUBLANES, D_CONV))] * (rb // SUBLANES)
        for k in range(CONV_WIDTH):
            b = (CONV_PAD + k) % SUBLANES
            wk = wb_ref[k]
            for rg in range(rb // SUBLANES):
                row = pl.multiple_of(r0 + (CONV_PAD + k - b) + rg * SUBLANES, SUBLANES)
                src = ext_ref[pl.ds(row, SUBLANES), :] if b == 0 else sh_ref[b - 1, pl.ds(row, SUBLANES), :]
                accs[rg] = accs[rg] + src * wk
        for rg in range(rb // SUBLANES):
            y_ref[pl.ds(pl.multiple_of(r0 + rg * SUBLANES, SUBLANES), SUBLANES), :] = accs[rg]
        return carry

    lax.fori_loop(0, tile // rb, row_block, 0)
    c_ref[...] = _ln_swish(y_ref[...], lg_ref[...], lb_ref[...]).astype(BF16)
    ext_ref[0:CONV_HALO, :] = ext_ref[tile:tile + CONV_HALO, :]


def _conv_prompt(u_all, conv_wb, conv_b3, ln_g3, ln_b3, l, batch, seq, tile):
    nt = seq // tile
    vec = pl.BlockSpec((None, 1, D_CONV), lambda b, j: (l, 0, 0))
    return pl.pallas_call(
        functools.partial(_conv_prompt_kernel, tile=tile, rb=32),
        grid=(batch, nt),
        in_specs=[
            pl.BlockSpec((tile, D_CONV), lambda b, j: (b * nt + j, 0)),
            pl.BlockSpec((None, CONV_WIDTH, SUBLANES, D_CONV), lambda b, j: (l, 0, 0, 0)),
            vec, vec, vec,
        ],
        out_specs=pl.BlockSpec((tile, D_CONV), lambda b, j: (b * nt + j, 0)),
        out_shape=jax.ShapeDtypeStruct((batch * seq, D_CONV), BF16),
        scratch_shapes=[pltpu.VMEM((tile + CONV_HALO, D_CONV), F32),
                        pltpu.VMEM((SUBLANES - 1, tile + CONV_HALO - SUBLANES, D_CONV), F32),
                        pltpu.VMEM((tile, D_CONV), F32)],
        compiler_params=_params("arbitrary", "arbitrary"),
        name="conv_prompt",
    )(u_all, conv_wb, conv_b3, ln_g3, ln_b3)


def _conv_sample_kernel(hist_ref, u_ref, wb_ref, cb_ref, lg_ref, lb_ref, c_ref):
    sb, steps, _ = u_ref.shape
    hist = CONV_WIDTH - 1
    accs = [jnp.broadcast_to(cb_ref[...], (sb, D_CONV)) for _ in range(steps)]
    for r in range(hist + steps):
        row = hist_ref[:, r, :] if r < hist else u_ref[:, r - hist, :]
        for t in range(steps):
            if 0 <= r - t < CONV_WIDTH:
                accs[t] = accs[t] + row * wb_ref[r - t]
    for t in range(steps):
        c_ref[:, t, :] = _ln_swish(accs[t], lg_ref[...], lb_ref[...]).astype(BF16)


def _conv_sample(cache_conv, u_s, conv_wb, conv_b3, ln_g3, ln_b3, l):
    db, steps, _ = u_s.shape
    sb = SUBLANES
    vec = pl.BlockSpec((None, 1, D_CONV), lambda i: (l, 0, 0))
    return pl.pallas_call(
        _conv_sample_kernel,
        grid=(db // sb,),
        in_specs=[
            pl.BlockSpec((None, sb, CONV_WIDTH - 1, D_CONV), lambda i: (l, i, 0, 0)),
            pl.BlockSpec((sb, steps, D_CONV), lambda i: (i, 0, 0)),
            pl.BlockSpec((None, CONV_WIDTH, SUBLANES, D_CONV), lambda i: (l, 0, 0, 0)),
            vec, vec, vec,
        ],
        out_specs=pl.BlockSpec((sb, steps, D_CONV), lambda i: (i, 0, 0)),
        out_shape=jax.ShapeDtypeStruct((db, steps, D_CONV), BF16),
        compiler_params=_params("parallel"),
        name="conv_sample",
    )(cache_conv, u_s, conv_wb, conv_b3, ln_g3, ln_b3)


def _softmax_sink_pv(s, mask, sink, v_bf, dot_pv):
    s = jnp.where(mask, s, -jnp.inf)
    m = jnp.maximum(jnp.max(s, -1, keepdims=True), sink)
    p = jnp.exp(s - m)
    den = jnp.sum(p, -1, keepdims=True) + jnp.exp(sink - m)
    return dot_pv(p.astype(BF16), v_bf) / den


def _attn_prompt_kernel(sink_ref, q_ref, kp_ref, kc_ref, vp_ref, vc_ref, o_ref, *, l, tq):
    first = pl.program_id(1) == 0
    cols = GROUP * BLOCK
    kj = lax.broadcasted_iota(I32, (2 * BLOCK, cols), 0)
    qi = lax.broadcasted_iota(I32, (2 * BLOCK, cols), 1) % BLOCK
    band = (kj > qi) & (kj <= qi + BLOCK)
    head_of_col = lax.broadcasted_iota(I32, (1, cols), 1) // BLOCK
    for qb in range(tq // BLOCK):
        if qb == 0:
            kk = jnp.concatenate([kp_ref[...], kc_ref[0:BLOCK, :]], 0)
            vv = jnp.concatenate([vp_ref[...], vc_ref[0:BLOCK, :]], 0)
            mask = band & (kj >= jnp.where(first, BLOCK, 0))
        else:
            kk = kc_ref[(qb - 1) * BLOCK:(qb + 1) * BLOCK, :]
            vv = vc_ref[(qb - 1) * BLOCK:(qb + 1) * BLOCK, :]
            mask = band
        kk = kk.astype(BF16)
        q = q_ref[qb * BLOCK:(qb + 1) * BLOCK, :]
        for g in range(N_KV_HEADS):
            hs = [g * GROUP + i for i in range(GROUP)]
            q4 = jnp.concatenate([q[:, h * HEAD_DIM:(h + 1) * HEAD_DIM] for h in hs], 0)
            sink = jnp.zeros((1, cols), F32)
            for i, h in enumerate(hs):
                sink = jnp.where(head_of_col == i, sink_ref[l, h], sink)
            s = lax.dot_general(kk[:, g * HEAD_DIM:(g + 1) * HEAD_DIM], q4, (((1,), (1,)), ((), ())),
                                preferred_element_type=F32)
            s = jnp.where(mask, s, -jnp.inf)
            m = jnp.maximum(jnp.max(s, 0, keepdims=True), sink)
            p = jnp.exp(s - m)
            den = jnp.sum(p, 0, keepdims=True) + jnp.exp(sink - m)
            v_t = vv[:, g * HEAD_DIM:(g + 1) * HEAD_DIM].T.astype(BF16)
            o_t = jnp.dot(v_t, p.astype(BF16), preferred_element_type=F32) * (1.0 / den)
            for i, h in enumerate(hs):
                o_ref[qb * BLOCK:(qb + 1) * BLOCK, h * HEAD_DIM:(h + 1) * HEAD_DIM] = (
                    o_t[:, i * BLOCK:(i + 1) * BLOCK].T.astype(BF16))


def _attn_prompt(sinks, q_all, k_all, v_all, l, batch, seq, tq):
    nq = seq // tq
    per = tq // BLOCK
    cur = lambda b, j: (b * nq + j, 0)
    prev = lambda b, j: (jnp.maximum((b * nq + j) * per - 1, 0), 0)
    return pl.pallas_call(
        functools.partial(_attn_prompt_kernel, l=l, tq=tq),
        grid=(batch, nq),
        in_specs=[
            pl.BlockSpec(memory_space=pltpu.SMEM),
            pl.BlockSpec((tq, D_ATTN), cur),
            pl.BlockSpec((BLOCK, D_KV), prev),
            pl.BlockSpec((tq, D_KV), cur),
            pl.BlockSpec((BLOCK, D_KV), prev),
            pl.BlockSpec((tq, D_KV), cur),
        ],
        out_specs=pl.BlockSpec((tq, D_ATTN), cur),
        out_shape=jax.ShapeDtypeStruct((batch * seq, D_ATTN), BF16),
        compiler_params=_params("parallel", "parallel"),
        name="attn_prompt",
    )(sinks, q_all, k_all, k_all, v_all, v_all)


def _attn_sample_kernel(sink_ref, q_ref, kn_ref, vn_ref, kc_ref, vc_ref, o_ref, *, l, steps):
    sb = q_ref.shape[0]
    hist = kc_ref.shape[1]
    rows = GROUP * steps
    pad = jnp.zeros((sb, hist - steps, D_KV), F32)
    kk = jnp.concatenate([kc_ref[...], kn_ref[...], pad], 1).astype(BF16)
    vv = jnp.concatenate([vc_ref[...], vn_ref[...], pad], 1).astype(BF16)
    t = lax.broadcasted_iota(I32, (sb, rows, 2 * hist), 1) % steps
    j = lax.broadcasted_iota(I32, (sb, rows, 2 * hist), 2)
    mask = ((j < hist) & (t + hist - j < WINDOW)) | ((j >= hist) & (j - hist <= t) & (t - (j - hist) < WINDOW))
    head_of_row = lax.broadcasted_iota(I32, (rows, 1), 0) // steps
    dot_qk = lambda a, b: jnp.einsum("bqd,bkd->bqk", a, b, preferred_element_type=F32)
    dot_pv = lambda a, b: jnp.einsum("bqk,bkd->bqd", a, b, preferred_element_type=F32)
    q = q_ref[...]
    for g in range(N_KV_HEADS):
        hs = [g * GROUP + i for i in range(GROUP)]
        q4 = jnp.concatenate([q[:, :, h * HEAD_DIM:(h + 1) * HEAD_DIM] for h in hs], 1)
        sink = jnp.zeros((rows, 1), F32)
        for i, h in enumerate(hs):
            sink = jnp.where(head_of_row == i, sink_ref[l, h], sink)
        s = dot_qk(q4, kk[:, :, g * HEAD_DIM:(g + 1) * HEAD_DIM])
        o = _softmax_sink_pv(s, mask, sink[None], vv[:, :, g * HEAD_DIM:(g + 1) * HEAD_DIM], dot_pv)
        for i, h in enumerate(hs):
            o_ref[:, :, h * HEAD_DIM:(h + 1) * HEAD_DIM] = o[:, i * steps:(i + 1) * steps, :].astype(BF16)


def _attn_sample(sinks, q_s, k_new, v_new, cache_k4, cache_v4, l, sb):
    db, steps, _ = q_s.shape
    hist = cache_k4.shape[2]
    seq3 = lambda i: (i, 0, 0)
    return pl.pallas_call(
        functools.partial(_attn_sample_kernel, l=l, steps=steps),
        grid=(db // sb,),
        in_specs=[
            pl.BlockSpec(memory_space=pltpu.SMEM),
            pl.BlockSpec((sb, steps, D_ATTN), seq3),
            pl.BlockSpec((sb, steps, D_KV), seq3),
            pl.BlockSpec((sb, steps, D_KV), seq3),
            pl.BlockSpec((None, sb, hist, D_KV), lambda i: (l, i, 0, 0)),
            pl.BlockSpec((None, sb, hist, D_KV), lambda i: (l, i, 0, 0)),
        ],
        out_specs=pl.BlockSpec((sb, steps, D_ATTN), seq3),
        out_shape=jax.ShapeDtypeStruct((db, steps, D_ATTN), BF16),
        compiler_params=_params("parallel"),
        name="attn_sample",
    )(sinks, q_s, k_new, v_new, cache_k4, cache_v4)


def _out_proj_kernel(c_ref, a_ref, w_ref, x_ref, o_ref):
    o_ref[...] = (x_ref[...]
                  + jnp.dot(c_ref[...], w_ref[0:D_CONV, :], preferred_element_type=F32)
                  + jnp.dot(a_ref[...], w_ref[D_CONV:, :], preferred_element_type=F32))


def _out_proj(c, a, w_out_bf, x_all, l, row0, tile):
    rows = c.shape[0]
    off = row0 // tile
    return pl.pallas_call(
        _out_proj_kernel,
        grid=(rows // tile,),
        in_specs=[
            pl.BlockSpec((tile, D_CONV), lambda i: (i, 0)),
            pl.BlockSpec((tile, D_ATTN), lambda i: (i, 0)),
            pl.BlockSpec((None, D_MODEL, D_MODEL), lambda i: (l, 0, 0)),
            pl.BlockSpec((tile, D_MODEL), lambda i: (off + i, 0)),
        ],
        out_specs=pl.BlockSpec((tile, D_MODEL), lambda i: (off + i, 0)),
        out_shape=jax.ShapeDtypeStruct(x_all.shape, F32),
        input_output_aliases={3: 0},
        compiler_params=_params("parallel"),
        name="out_proj",
    )(c, a, w_out_bf, x_all)


FF_CHUNK = 256


def _swiglu(x, g_ref, wgu_ref, wd_ref, act_ref):
    h = _rms(x, g_ref[...]).astype(BF16)
    for c0 in range(0, D_FF, FF_CHUNK):
        gate = jnp.dot(h, wgu_ref[:, c0:c0 + FF_CHUNK], preferred_element_type=F32)
        up = jnp.dot(h, wgu_ref[:, D_FF + c0:D_FF + c0 + FF_CHUNK], preferred_element_type=F32)
        act_ref[:, c0:c0 + FF_CHUNK] = (gate * _sigmoid(gate) * up).astype(BF16)
    return jnp.dot(act_ref[...], wd_ref[...], preferred_element_type=F32)


def _ffn_dense_kernel(x_ref, g_ref, wgu_ref, wd_ref, o_ref, act_ref):
    x = x_ref[...]
    o_ref[...] = x + _swiglu(x, g_ref, wgu_ref, wd_ref, act_ref)


def _ffn_dense(x_all, g_ffn3, wgu_bf, wd_bf, l, tile):
    n = x_all.shape[0]
    once = pl.Buffered(1)
    return pl.pallas_call(
        _ffn_dense_kernel,
        grid=(n // tile,),
        in_specs=[
            pl.BlockSpec((tile, D_MODEL), lambda i: (i, 0)),
            pl.BlockSpec((None, 1, D_MODEL), lambda i: (l, 0, 0)),
            pl.BlockSpec((None, D_MODEL, 2 * D_FF), lambda i: (l // 2, 0, 0), pipeline_mode=once),
            pl.BlockSpec((None, D_FF, D_MODEL), lambda i: (l // 2, 0, 0), pipeline_mode=once),
        ],
        out_specs=pl.BlockSpec((tile, D_MODEL), lambda i: (i, 0)),
        out_shape=jax.ShapeDtypeStruct(x_all.shape, F32),
        scratch_shapes=[pltpu.VMEM((tile, D_FF), BF16)],
        input_output_aliases={0: 0},
        compiler_params=_params("parallel"),
        name="ffn_dense",
    )(x_all, g_ffn3, wgu_bf, wd_bf)


INFO_E, INFO_G, INFO_R = 0, 2, 4


def _route_kernel(x_ref, g_ref, whl_ref, info_ref, cnt_ref, run_ref):
    @pl.when(pl.program_id(0) == 0)
    def _():
        run_ref[...] = jnp.zeros_like(run_ref)

    h = _rms(x_ref[...], g_ref[...])
    h_hi = h.astype(BF16)
    h_lo = (h - h_hi.astype(F32)).astype(BF16)
    dot = lambda a, b: jnp.dot(a, b, preferred_element_type=F32)
    hh = dot(h_hi, whl_ref[...])
    logits = hh[:, :LANES] + hh[:, LANES:] + dot(h_lo, whl_ref[:, :LANES])
    t = logits.shape[0]
    lane = lax.broadcasted_iota(I32, logits.shape, 1)
    lg = jnp.where(lane < N_EXPERTS, logits, -jnp.inf)
    m1 = jnp.max(lg, -1, keepdims=True)
    i1 = jnp.min(jnp.where(lg == m1, lane, LANES), -1, keepdims=True)
    lg2 = jnp.where(lane == i1, -jnp.inf, lg)
    m2 = jnp.max(lg2, -1, keepdims=True)
    i2 = jnp.min(jnp.where(lg2 == m2, lane, LANES), -1, keepdims=True)
    e = jnp.exp(m2 - m1)
    g1 = 1.0 / (1.0 + e)
    g2 = e / (1.0 + e)
    sel1 = lane == i1
    sel2 = lane == i2
    onehot = jnp.where(sel1 | sel2, 1.0, 0.0)
    r = lax.broadcasted_iota(I32, (t, t), 0)
    c = lax.broadcasted_iota(I32, (t, t), 1)
    tri = jnp.where(r > c, 1.0, 0.0).astype(BF16)
    before = dot(tri, onehot.astype(BF16)) + run_ref[0:1, :]
    r1 = jnp.sum(jnp.where(sel1, before, 0.0), -1, keepdims=True)
    r2 = jnp.sum(jnp.where(sel2, before, 0.0), -1, keepdims=True)
    run_ref[...] = run_ref[...] + jnp.sum(onehot, 0, keepdims=True)
    cnt_ref[...] = run_ref[...]
    info = jnp.zeros(logits.shape, F32)
    for pos, val in ((INFO_E, i1.astype(F32)), (INFO_E + 1, i2.astype(F32)), (INFO_G, g1), (INFO_G + 1, g2),
                     (INFO_R, r1), (INFO_R + 1, r2)):
        info = jnp.where(lane == pos, val, info)
    info_ref[...] = info


def _route(x_all, g_ffn3, wr_hl, l, tile):
    n = x_all.shape[0]
    return pl.pallas_call(
        _route_kernel,
        grid=(n // tile,),
        in_specs=[
            pl.BlockSpec((tile, D_MODEL), lambda i: (i, 0)),
            pl.BlockSpec((None, 1, D_MODEL), lambda i: (l, 0, 0)),
            pl.BlockSpec((None, D_MODEL, 2 * LANES), lambda i: (l // 2, 0, 0)),
        ],
        out_specs=[
            pl.BlockSpec((tile, LANES), lambda i: (i, 0)),
            pl.BlockSpec((SUBLANES, LANES), lambda i: (0, 0)),
        ],
        out_shape=[
            jax.ShapeDtypeStruct((n, LANES), F32),
            jax.ShapeDtypeStruct((SUBLANES, LANES), F32),
        ],
        scratch_shapes=[pltpu.VMEM((SUBLANES, LANES), F32)],
        compiler_params=_params("arbitrary"),
        name="route",
    )(x_all, g_ffn3, wr_hl)


def _row_copy(src_ref, src_row, dst_ref, dst_row, sem):
    return pltpu.make_async_copy(src_ref.at[pl.ds(src_row, 1)], dst_ref.at[pl.ds(dst_row, 1)], sem)


def _start_row_copies(tile, make_copy):
    def body(group, carry):
        for row in range(SUBLANES):
            for j in range(2):
                make_copy(group, row, j).start(priority=row % 2)
        return carry

    lax.fori_loop(0, tile // SUBLANES, body, 0)


CAST_PARTS = 8


def _dispatch_kernel(pad_start_ref, pad_len_ref, pos_ref, x_ref, wgu_ref, wd_ref, xs_ref, wgu_bf_ref, wd_bf_ref,
                     zero_ref, sem, zsem, *, tile, tm):
    first = pl.program_id(0) == 0
    wgu_bf_ref[...] = wgu_ref[...].astype(BF16)
    wd_bf_ref[...] = wd_ref[...].astype(BF16)

    def pad_copies(action):
        for e in range(N_EXPERTS):
            start = pad_start_ref[e]
            lead = pad_len_ref[e] & (SUBLANES - 1)
            for r in range(SUBLANES - 1):
                @pl.when(r < lead)
                def _():
                    action(_row_copy(zero_ref, 0, xs_ref, start + r, zsem))

            start = pl.multiple_of(start + lead, SUBLANES)
            run = tm // 2
            while run >= SUBLANES:
                take = (pad_len_ref[e] & run) != 0

                @pl.when(take)
                def _():
                    action(pltpu.make_async_copy(zero_ref.at[pl.ds(0, run)], xs_ref.at[pl.ds(start, run)], zsem))

                start = pl.multiple_of(start + jnp.where(take, run, 0), SUBLANES)
                run //= 2
        n_tiles = xs_ref.shape[0] // tm
        for t in range(n_tiles - (N_EXPERTS - 1), n_tiles):
            @pl.when(t >= pad_len_ref[N_EXPERTS])
            def _():
                for half in range(2):
                    action(pltpu.make_async_copy(zero_ref, xs_ref.at[pl.ds(t * tm + half * (tm // 2), tm // 2)], zsem))

    @pl.when(first)
    def _():
        zero_ref[...] = jnp.zeros_like(zero_ref)
        pad_copies(lambda c: c.start())

    _start_row_copies(tile, lambda grp, row, j: pltpu.make_async_copy(
        x_ref.at[grp, pl.ds(row, 1)], xs_ref.at[pl.ds(pos_ref[0, j, grp * SUBLANES + row], 1)], sem.at[j]))
    for j in range(2):
        pltpu.make_async_copy(xs_ref.at[pl.ds(0, tile)], xs_ref.at[pl.ds(0, tile)], sem.at[j]).wait()

    @pl.when(first)
    def _():
        pad_copies(lambda c: c.wait())


def _dispatch(pad_start, pad_len, pos3, x_all, w_gu, w_down, l, n_rows, tile, tm):
    n = x_all.shape[0]
    steps = n // tile
    assert steps >= N_EXPERTS * CAST_PARTS
    m = l // 2
    gu_rows, d_rows = D_MODEL // CAST_PARTS, D_FF // CAST_PARTS

    def part(i):
        c = jnp.minimum(i, N_EXPERTS * CAST_PARTS - 1)
        return c // CAST_PARTS, c % CAST_PARTS

    grid_spec = pltpu.PrefetchScalarGridSpec(
        num_scalar_prefetch=2,
        grid=(steps,),
        in_specs=[
            pl.BlockSpec((1, 2, tile), lambda i, ps, pn: (i, 0, 0), memory_space=pltpu.SMEM),
            pl.BlockSpec((tile // SUBLANES, SUBLANES, D_MODEL), lambda i, ps, pn: (i, 0, 0)),
            pl.BlockSpec((None, None, gu_rows, 2 * D_FF), lambda i, ps, pn: (m, *part(i), 0)),
            pl.BlockSpec((None, None, d_rows, D_MODEL), lambda i, ps, pn: (m, *part(i), 0)),
        ],
        out_specs=[
            pl.BlockSpec(memory_space=pl.ANY),
            pl.BlockSpec((None, gu_rows, 2 * D_FF), lambda i, ps, pn: (*part(i), 0)),
            pl.BlockSpec((None, d_rows, D_MODEL), lambda i, ps, pn: (*part(i), 0)),
        ],
        scratch_shapes=[pltpu.VMEM((tm // 2, D_MODEL), F32), pltpu.SemaphoreType.DMA((2,)),
                        pltpu.SemaphoreType.DMA(())],
    )
    return pl.pallas_call(
        functools.partial(_dispatch_kernel, tile=tile, tm=tm),
        grid_spec=grid_spec,
        out_shape=[
            jax.ShapeDtypeStruct((n_rows, D_MODEL), F32),
            jax.ShapeDtypeStruct((N_EXPERTS, D_MODEL, 2 * D_FF), BF16),
            jax.ShapeDtypeStruct((N_EXPERTS, D_FF, D_MODEL), BF16),
        ],
        compiler_params=_params("arbitrary"),
        name="moe_dispatch",
    )(pad_start, pad_len, pos3, x_all.reshape(n // SUBLANES, SUBLANES, D_MODEL), w_gu, w_down)


def _moe_kernel(te_ref, used_ref, xs_ref, g_ref, wgu_ref, wd_ref, y_ref, act_ref):
    used = pl.program_id(0) < used_ref[0]

    @pl.when(used)
    def _():
        y_ref[...] = _swiglu(xs_ref[...], g_ref, wgu_ref, wd_ref, act_ref)

    @pl.when(jnp.logical_not(used))
    def _():
        y_ref[...] = jnp.zeros_like(y_ref)


def _moe(tile_expert, n_used, xs, g_ffn3, wgu_bf, wd_bf, l, tm):
    nt = xs.shape[0] // tm

    def tile_idx(t, used):
        return jnp.minimum(t, used[0] - 1)

    grid_spec = pltpu.PrefetchScalarGridSpec(
        num_scalar_prefetch=2,
        grid=(nt,),
        in_specs=[
            pl.BlockSpec((tm, D_MODEL), lambda t, te, used: (tile_idx(t, used), 0)),
            pl.BlockSpec((None, 1, D_MODEL), lambda t, te, used: (l, 0, 0)),
            pl.BlockSpec((None, D_MODEL, 2 * D_FF), lambda t, te, used: (te[tile_idx(t, used)], 0, 0)),
            pl.BlockSpec((None, D_FF, D_MODEL), lambda t, te, used: (te[tile_idx(t, used)], 0, 0)),
        ],
        out_specs=pl.BlockSpec((tm, D_MODEL), lambda t, te, used: (t, 0)),
        scratch_shapes=[pltpu.VMEM((tm, D_FF), BF16)],
    )
    return pl.pallas_call(
        _moe_kernel,
        grid_spec=grid_spec,
        out_shape=jax.ShapeDtypeStruct(xs.shape, F32),
        compiler_params=_params("arbitrary"),
        name="moe_experts",
    )(tile_expert, n_used, xs, g_ffn3, wgu_bf, wd_bf)


def _gather_combine(pos_ref, info_ref, x_ref, y_ref, buf_ref, sem, tile):
    _start_row_copies(tile, lambda grp, row, j: pltpu.make_async_copy(
        y_ref.at[pl.ds(pos_ref[0, j, grp * SUBLANES + row], 1)], buf_ref.at[j, grp, pl.ds(row, 1)], sem.at[j]))
    for j in range(2):
        pltpu.make_async_copy(y_ref.at[pl.ds(0, tile)], y_ref.at[pl.ds(0, tile)], sem.at[j]).wait()
    info = info_ref[...]
    return (x_ref[...]
            + info[:, INFO_G:INFO_G + 1] * buf_ref[0].reshape(tile, D_MODEL)
            + info[:, INFO_G + 1:INFO_G + 2] * buf_ref[1].reshape(tile, D_MODEL))


def _combine_kernel(pos_ref, info_ref, x_ref, y_ref, o_ref, buf_ref, sem, *, tile):
    o_ref[...] = _gather_combine(pos_ref, info_ref, x_ref, y_ref, buf_ref, sem, tile)


def _combine_final_kernel(pos_ref, info_ref, x_ref, y_ref, g_ref, op_ref, os_ref, buf_ref, sem, *, tile, steps_p):
    out = _rms(_gather_combine(pos_ref, info_ref, x_ref, y_ref, buf_ref, sem, tile), g_ref[...])
    is_prompt = pl.program_id(0) < steps_p

    @pl.when(is_prompt)
    def _():
        op_ref[...] = out

    @pl.when(jnp.logical_not(is_prompt))
    def _():
        os_ref[...] = out


def _combine(pos3, info, x_all, y, tile, final=None):
    n = x_all.shape[0]
    row = lambda i: (i, 0)
    in_specs = [
        pl.BlockSpec((1, 2, tile), lambda i: (i, 0, 0), memory_space=pltpu.SMEM),
        pl.BlockSpec((tile, LANES), row),
        pl.BlockSpec((tile, D_MODEL), row),
        pl.BlockSpec(memory_space=pl.ANY),
    ]
    scratch = [pltpu.VMEM((2, tile // SUBLANES, SUBLANES, D_MODEL), F32), pltpu.SemaphoreType.DMA((2,))]
    if final is None:
        return pl.pallas_call(
            functools.partial(_combine_kernel, tile=tile),
            grid=(n // tile,),
            in_specs=in_specs,
            out_specs=pl.BlockSpec((tile, D_MODEL), row),
            out_shape=jax.ShapeDtypeStruct(x_all.shape, F32),
            scratch_shapes=scratch,
            input_output_aliases={2: 0},
            compiler_params=_params("arbitrary"),
            name="moe_combine",
        )(pos3, info, x_all, y)
    g_fin, n_p = final
    steps_p = n_p // tile
    return pl.pallas_call(
        functools.partial(_combine_final_kernel, tile=tile, steps_p=steps_p),
        grid=(n // tile,),
        in_specs=in_specs + [pl.BlockSpec((1, D_MODEL), lambda i: (0, 0))],
        out_specs=[
            pl.BlockSpec((tile, D_MODEL), lambda i: (jnp.minimum(i, steps_p - 1), 0)),
            pl.BlockSpec((tile, D_MODEL), lambda i: (jnp.maximum(i - steps_p, 0), 0)),
        ],
        out_shape=[jax.ShapeDtypeStruct((n_p, D_MODEL), F32), jax.ShapeDtypeStruct((n - n_p, D_MODEL), F32)],
        scratch_shapes=scratch,
        compiler_params=_params("arbitrary"),
        name="moe_combine_final",
    )(pos3, info, x_all, y, g_fin)


def _ffn_moe(x_all, g_ffn3, wr_hl, w_moe_gu, w_moe_down, l, tile, tm, final=None):
    n = x_all.shape[0]
    td = next(t for t in (256, 128, 64, 32, 16, 8) if n % t == 0 and n // t >= N_EXPERTS * CAST_PARTS)
    info, cnt = _route(x_all, g_ffn3, wr_hl, l, tile)
    counts = cnt[0, :N_EXPERTS].astype(I32)
    tiles_per = (counts + tm - 1) // tm
    tile_end = jnp.cumsum(tiles_per)
    base = (tile_end - tiles_per) * tm
    experts = info[:, INFO_E:INFO_E + 2].astype(I32)
    ranks = info[:, INFO_R:INFO_R + 2].astype(I32)
    pos = jnp.sum(jnp.where(experts[..., None] == jnp.arange(N_EXPERTS), base, 0), -1) + ranks
    n_tiles = (2 * n + N_EXPERTS * (tm - 1)) // tm
    n_used = tile_end[-1:]
    tile_expert = jnp.minimum(jnp.sum(jnp.arange(n_tiles)[:, None] >= tile_end[None, :], -1), N_EXPERTS - 1)
    blocked = lambda t: pos.T.reshape(2, n // t, t).transpose(1, 0, 2)
    pad_len = jnp.concatenate([tiles_per * tm - counts, n_used])
    xs, wgu_bf, wd_bf = _dispatch(base + counts, pad_len, blocked(td), x_all, w_moe_gu, w_moe_down, l,
                                  n_tiles * tm, td, tm)
    y = _moe(tile_expert.astype(I32), n_used.astype(I32), xs, g_ffn3, wgu_bf, wd_bf, l, tm)
    return _combine(blocked(tile), info, x_all, y, tile, final)


def _final_kernel(x_ref, g_ref, o_ref):
    o_ref[...] = _rms(x_ref[...], g_ref[...])


def _final_norm(x_all, g2, row0, rows, tile):
    off = row0 // tile
    return pl.pallas_call(
        _final_kernel,
        grid=(rows // tile,),
        in_specs=[
            pl.BlockSpec((tile, D_MODEL), lambda i: (off + i, 0)),
            pl.BlockSpec((1, D_MODEL), lambda i: (0, 0)),
        ],
        out_specs=pl.BlockSpec((tile, D_MODEL), lambda i: (i, 0)),
        out_shape=jax.ShapeDtypeStruct((rows, D_MODEL), F32),
        compiler_params=_params("parallel"),
        name="final_norm",
    )(x_all, g2)


def _rope_tables(pos):
    half = HEAD_DIM // 2
    inv = ROPE_THETA ** (-jnp.arange(half, dtype=F32) / half)
    ang = pos.astype(F32)[:, None] * inv[None, :]
    cos = jnp.cos(ang)
    sin = jnp.sin(ang)
    reps = LANES // HEAD_DIM
    return jnp.tile(jnp.concatenate([cos, cos], -1), (1, reps)), jnp.tile(jnp.concatenate([-sin, sin], -1), (1, reps))


def kernel(x_prompt, x_sample, cache_conv, cache_swa_k, cache_swa_v, g_mix, w_in, conv_w, conv_b, conv_ln_g,
           conv_ln_b, attn_sinks, w_out, g_ffn, w_dense_gu, w_dense_down, w_router, w_moe_gu, w_moe_down, g_final):
    batch, seq, _ = x_prompt.shape
    db, steps, _ = x_sample.shape
    depth = g_mix.shape[0]
    hist = cache_swa_k.shape[2]
    n_p, n_s = batch * seq, db * steps
    n = n_p + n_s
    assert seq % BLOCK == 0 and seq >= CONV_WIDTH - 1 and steps <= hist and steps < CONV_WIDTH - 1
    assert hist == min(WINDOW, PAST_LEN)
    tile = _pick_tile((1024, 512, 256, 128), n_p, n_s)
    tq = _pick_tile((512, 128), seq)
    tm = _pick_tile((512, 256), 2 * n)
    sb_attn = _pick_tile((16, 8), db)
    assert db % SUBLANES == 0
    conv_wb = jnp.broadcast_to(conv_w[:, :, None, :], (depth, CONV_WIDTH, SUBLANES, D_CONV))

    x_all = jnp.concatenate([x_prompt.reshape(n_p, D_MODEL), x_sample.reshape(n_s, D_MODEL)], 0)
    pos_all = jnp.concatenate([jnp.tile(jnp.arange(seq, dtype=I32), batch),
                               jnp.tile(PAST_LEN + jnp.arange(steps, dtype=I32), db)])
    cos_t, sin_t = _rope_tables(pos_all)

    vec3 = lambda a: a.reshape(a.shape[0], 1, a.shape[1])
    g_mix3, g_ffn3, conv_b3, ln_g3, ln_b3 = map(vec3, (g_mix, g_ffn, conv_b, conv_ln_g, conv_ln_b))
    w_in_bf, w_out_bf = w_in.astype(BF16), w_out.astype(BF16)
    wgu_bf, wd_bf = w_dense_gu.astype(BF16), w_dense_down.astype(BF16)
    wr = jnp.pad(w_router, ((0, 0), (0, 0), (0, LANES - N_EXPERTS)))
    wr_hi = wr.astype(BF16)
    wr_hl = jnp.concatenate([wr_hi, (wr - wr_hi.astype(F32)).astype(BF16)], -1)
    cache_k4 = cache_swa_k.reshape(depth, db, hist, D_KV)
    cache_v4 = cache_swa_v.reshape(depth, db, hist, D_KV)

    def tail_rows(a, rows):
        return jnp.stack([a[(b + 1) * seq - rows:(b + 1) * seq] for b in range(batch)], 0)

    keep = min(WINDOW, seq)
    g_fin = g_final.reshape(1, D_MODEL)
    conv_p, k_p, v_p, u_new, k_new, v_new = [], [], [], [], [], []
    for l in range(depth):
        u, q, k, v = _in_proj(x_all, g_mix3, w_in_bf, cos_t, sin_t, l, tile)

        c_p = _conv_prompt(u, conv_wb, conv_b3, ln_g3, ln_b3, l, batch, seq, tq)
        a_p = _attn_prompt(attn_sinks, q, k, v, l, batch, seq, tq)
        x_all = _out_proj(c_p, a_p, w_out_bf, x_all, l, 0, tile)

        u_s = u[n_p:].reshape(db, steps, D_CONV)
        k_s = k[n_p:].reshape(db, steps, D_KV)
        v_s = v[n_p:].reshape(db, steps, D_KV)
        c_s = _conv_sample(cache_conv, u_s, conv_wb, conv_b3, ln_g3, ln_b3, l).reshape(n_s, D_CONV)
        a_s = _attn_sample(attn_sinks, q[n_p:].reshape(db, steps, D_ATTN), k_s, v_s, cache_k4, cache_v4,
                           l, sb_attn).reshape(n_s, D_ATTN)
        x_all = _out_proj(c_s, a_s, w_out_bf, x_all, l, n_p, tile)

        conv_p.append(tail_rows(u, CONV_WIDTH - 1))
        k_p.append(tail_rows(k, keep).reshape(batch, keep, N_KV_HEADS, HEAD_DIM))
        v_p.append(tail_rows(v, keep).reshape(batch, keep, N_KV_HEADS, HEAD_DIM))
        u_new.append(u_s)
        k_new.append(k_s.reshape(db, steps, N_KV_HEADS, HEAD_DIM))
        v_new.append(v_s.reshape(db, steps, N_KV_HEADS, HEAD_DIM))

        if l % 2 == 0:
            x_all = _ffn_dense(x_all, g_ffn3, wgu_bf, wd_bf, l, tile)
        elif l < depth - 1:
            x_all = _ffn_moe(x_all, g_ffn3, wr_hl, w_moe_gu, w_moe_down, l, tile, tm)
        else:
            y_p, y_s = _ffn_moe(x_all, g_ffn3, wr_hl, w_moe_gu, w_moe_down, l, tile, tm, final=(g_fin, n_p))

    if depth % 2:
        y_p, y_s = _final_norm(x_all, g_fin, 0, n_p, tile), _final_norm(x_all, g_fin, n_p, n_s, tile)
    y_prompt = y_p.reshape(batch, seq, D_MODEL)
    y_sample = y_s.reshape(db, steps, D_MODEL)
    state_s = lambda cache, new: jnp.concatenate([cache[:, :, steps:], jnp.stack(new, 0)], 2)
    return (y_prompt, y_sample, jnp.stack(conv_p, 0), jnp.stack(k_p, 0), jnp.stack(v_p, 0),
            state_s(cache_conv, u_new), state_s(cache_swa_k, k_new), state_s(cache_swa_v, v_new))
```

```python
import functools

import jax
import jax.numpy as jnp
from jax import lax
from jax.experimental import pallas as pl
from jax.experimental.pallas import tpu as pltpu

F32 = jnp.float32
BF16 = jnp.bfloat16
I32 = jnp.int32

D_MODEL = 1024
D_CONV = 512
N_HEADS = 8
HEAD_DIM = 64
N_KV_HEADS = 2
GROUP = N_HEADS // N_KV_HEADS
D_ATTN = N_HEADS * HEAD_DIM
D_KV = N_KV_HEADS * HEAD_DIM
D_IN = 2 * D_CONV + D_ATTN + 2 * D_KV
CONV_WIDTH = 31
WINDOW = 128
BLOCK = 128
ROPE_THETA = 10000.0
D_FF = 2816
N_EXPERTS = 8
EPS = 1e-6
PAST_LEN = 8192

LANES = 128
SUBLANES = 8
CONV_HALO = 32
CONV_PAD = CONV_HALO - (CONV_WIDTH - 1)
VMEM_LIMIT = 56 * 1024 * 1024


def _pick_tile(cands, *sizes):
    for c in cands:
        if all(s % c == 0 for s in sizes):
            return c
    raise ValueError(f"no tile in {cands} divides {sizes}")


def _params(*sem):
    return pltpu.CompilerParams(dimension_semantics=sem, vmem_limit_bytes=VMEM_LIMIT)


def _rms(x, g):
    return x * lax.rsqrt(jnp.mean(x * x, -1, keepdims=True) + EPS) * g


def _sigmoid(x):
    return 1.0 / (1.0 + jnp.exp(-x))


def _in_proj_kernel(x_ref, g_ref, w_ref, cos_ref, sin_ref, u_ref, q_ref, k_ref, v_ref):
    h = _rms(x_ref[...], g_ref[...]).astype(BF16)
    p = jnp.dot(h, w_ref[...], preferred_element_type=F32)
    u_ref[...] = p[:, :D_CONV] * _sigmoid(p[:, D_CONV:2 * D_CONV])
    cos = cos_ref[...]
    sin = sin_ref[...]
    lane = lax.broadcasted_iota(I32, cos.shape, 1)
    first_half = (lane % HEAD_DIM) < (HEAD_DIM // 2)

    def rope(xc):
        partner = jnp.where(first_half,
                            pltpu.roll(xc, LANES - HEAD_DIM // 2, 1),
                            pltpu.roll(xc, HEAD_DIM // 2, 1))
        return xc * cos + partner * sin

    q0 = 2 * D_CONV
    for c in range(D_ATTN // LANES):
        qc = rope(p[:, q0 + c * LANES:q0 + (c + 1) * LANES])
        q_ref[:, c * LANES:(c + 1) * LANES] = (qc * (HEAD_DIM ** -0.5)).astype(BF16)
    k0 = q0 + D_ATTN
    k_ref[...] = rope(p[:, k0:k0 + D_KV])
    v_ref[...] = p[:, k0 + D_KV:k0 + 2 * D_KV]


def _in_proj(x_all, g_mix3, w_in_bf, cos_t, sin_t, l, tile):
    n = x_all.shape[0]
    row = lambda i: (i, 0)
    return pl.pallas_call(
        _in_proj_kernel,
        grid=(n // tile,),
        in_specs=[
            pl.BlockSpec((tile, D_MODEL), row),
            pl.BlockSpec((None, 1, D_MODEL), lambda i: (l, 0, 0)),
            pl.BlockSpec((None, D_MODEL, D_IN), lambda i: (l, 0, 0)),
            pl.BlockSpec((tile, LANES), row),
            pl.BlockSpec((tile, LANES), row),
        ],
        out_specs=[
            pl.BlockSpec((tile, D_CONV), row),
            pl.BlockSpec((tile, D_ATTN), row),
            pl.BlockSpec((tile, D_KV), row),
            pl.BlockSpec((tile, D_KV), row),
        ],
        out_shape=[
            jax.ShapeDtypeStruct((n, D_CONV), F32),
            jax.ShapeDtypeStruct((n, D_ATTN), BF16),
            jax.ShapeDtypeStruct((n, D_KV), F32),
            jax.ShapeDtypeStruct((n, D_KV), F32),
        ],
        compiler_params=_params("parallel"),
        name="in_proj",
    )(x_all, g_mix3, w_in_bf, cos_t, sin_t)


def _ln_swish(y, g, b):
    mu = jnp.mean(y, -1, keepdims=True)
    yc = y - mu
    z = yc * lax.rsqrt(jnp.mean(yc * yc, -1, keepdims=True) + EPS) * g + b
    return z * _sigmoid(z)


def _conv_prompt_kernel(u_ref, wb_ref, cb_ref, lg_ref, lb_ref, c_ref, ext_ref, sh_ref, y_ref, *, tile, rb):
    @pl.when(pl.program_id(1) == 0)
    def _():
        ext_ref[0:CONV_HALO, :] = jnp.zeros((CONV_HALO, D_CONV), F32)

    ext_ref[CONV_HALO:CONV_HALO + tile, :] = u_ref[...]
    for b in range(1, SUBLANES):
        sh_ref[b - 1] = ext_ref[pl.ds(b, sh_ref.shape[1]), :]
    def row_block(i, carry):
        r0 = pl.multiple_of(i * rb, rb)
        accs = [jnp.broadcast_to(cb_ref[...], (SUBLANES, D_CONV))] * (rb // SUBLANES)
        for k in range(CONV_WIDTH):
            b = (CONV_PAD + k) % SUBLANES
            wk = wb_ref[k]
            for rg in range(rb // SUBLANES):
                row = pl.multiple_of(r0 + (CONV_PAD + k - b) + rg * SUBLANES, SUBLANES)
                src = ext_ref[pl.ds(row, SUBLANES), :] if b == 0 else sh_ref[b - 1, pl.ds(row, SUBLANES), :]
                accs[rg] = accs[rg] + src * wk
        for rg in range(rb // SUBLANES):
            y_ref[pl.ds(pl.multiple_of(r0 + rg * SUBLANES, SUBLANES), SUBLANES), :] = accs[rg]
        return carry

    lax.fori_loop(0, tile // rb, row_block, 0)
    c_ref[...] = _ln_swish(y_ref[...], lg_ref[...], lb_ref[...]).astype(BF16)
    ext_ref[0:CONV_HALO, :] = ext_ref[tile:tile + CONV_HALO, :]


def _conv_prompt(u_all, conv_wb, conv_b3, ln_g3, ln_b3, l, batch, seq, tile):
    nt = seq // tile
    vec = pl.BlockSpec((None, 1, D_CONV), lambda b, j: (l, 0, 0))
    return pl.pallas_call(
        functools.partial(_conv_prompt_kernel, tile=tile, rb=32),
        grid=(batch, nt),
        in_specs=[
            pl.BlockSpec((tile, D_CONV), lambda b, j: (b * nt + j, 0)),
            pl.BlockSpec((None, CONV_WIDTH, SUBLANES, D_CONV), lambda b, j: (l, 0, 0, 0)),
            vec, vec, vec,
        ],
        out_specs=pl.BlockSpec((tile, D_CONV), lambda b, j: (b * nt + j, 0)),
        out_shape=jax.ShapeDtypeStruct((batch * seq, D_CONV), BF16),
        scratch_shapes=[pltpu.VMEM((tile + CONV_HALO, D_CONV), F32),
                        pltpu.VMEM((SUBLANES - 1, tile + CONV_HALO - SUBLANES, D_CONV), F32),
                        pltpu.VMEM((tile, D_CONV), F32)],
        compiler_params=_params("arbitrary", "arbitrary"),
        name="conv_prompt",
    )(u_all, conv_wb, conv_b3, ln_g3, ln_b3)


def _conv_sample_kernel(hist_ref, u_ref, wb_ref, cb_ref, lg_ref, lb_ref, c_ref):
    sb, steps, _ = u_ref.shape
    hist = CONV_WIDTH - 1
    accs = [jnp.broadcast_to(cb_ref[...], (sb, D_CONV)) for _ in range(steps)]
    for r in range(hist + steps):
        row = hist_ref[:, r, :] if r < hist else u_ref[:, r - hist, :]
        for t in range(steps):
            if 0 <= r - t < CONV_WIDTH:
                accs[t] = accs[t] + row * wb_ref[r - t]
    for t in range(steps):
        c_ref[:, t, :] = _ln_swish(accs[t], lg_ref[...], lb_ref[...]).astype(BF16)


def _conv_sample(cache_conv, u_s, conv_wb, conv_b3, ln_g3, ln_b3, l):
    db, steps, _ = u_s.shape
    sb = SUBLANES
    vec = pl.BlockSpec((None, 1, D_CONV), lambda i: (l, 0, 0))
    return pl.pallas_call(
        _conv_sample_kernel,
        grid=(db // sb,),
        in_specs=[
            pl.BlockSpec((None, sb, CONV_WIDTH - 1, D_CONV), lambda i: (l, i, 0, 0)),
            pl.BlockSpec((sb, steps, D_CONV), lambda i: (i, 0, 0)),
            pl.BlockSpec((None, CONV_WIDTH, SUBLANES, D_CONV), lambda i: (l, 0, 0, 0)),
            vec, vec, vec,
        ],
        out_specs=pl.BlockSpec((sb, steps, D_CONV), lambda i: (i, 0, 0)),
        out_shape=jax.ShapeDtypeStruct((db, steps, D_CONV), BF16),
        compiler_params=_params("parallel"),
        name="conv_sample",
    )(cache_conv, u_s, conv_wb, conv_b3, ln_g3, ln_b3)


def _softmax_sink_pv(s, mask, sink, v_bf, dot_pv):
    s = jnp.where(mask, s, -jnp.inf)
    m = jnp.maximum(jnp.max(s, -1, keepdims=True), sink)
    p = jnp.exp(s - m)
    den = jnp.sum(p, -1, keepdims=True) + jnp.exp(sink - m)
    return dot_pv(p.astype(BF16), v_bf) / den


def _attn_prompt_kernel(sink_ref, q_ref, kp_ref, kc_ref, vp_ref, vc_ref, o_ref, *, l, tq):
    first = pl.program_id(1) == 0
    cols = GROUP * BLOCK
    kj = lax.broadcasted_iota(I32, (2 * BLOCK, cols), 0)
    qi = lax.broadcasted_iota(I32, (2 * BLOCK, cols), 1) % BLOCK
    band = (kj > qi) & (kj <= qi + BLOCK)
    head_of_col = lax.broadcasted_iota(I32, (1, cols), 1) // BLOCK
    for qb in range(tq // BLOCK):
        if qb == 0:
            kk = jnp.concatenate([kp_ref[...], kc_ref[0:BLOCK, :]], 0)
            vv = jnp.concatenate([vp_ref[...], vc_ref[0:BLOCK, :]], 0)
            mask = band & (kj >= jnp.where(first, BLOCK, 0))
        else:
            kk = kc_ref[(qb - 1) * BLOCK:(qb + 1) * BLOCK, :]
            vv = vc_ref[(qb - 1) * BLOCK:(qb + 1) * BLOCK, :]
            mask = band
        kk = kk.astype(BF16)
        q = q_ref[qb * BLOCK:(qb + 1) * BLOCK, :]
        for g in range(N_KV_HEADS):
            hs = [g * GROUP + i for i in range(GROUP)]
            q4 = jnp.concatenate([q[:, h * HEAD_DIM:(h + 1) * HEAD_DIM] for h in hs], 0)
            sink = jnp.zeros((1, cols), F32)
            for i, h in enumerate(hs):
                sink = jnp.where(head_of_col == i, sink_ref[l, h], sink)
            s = lax.dot_general(kk[:, g * HEAD_DIM:(g + 1) * HEAD_DIM], q4, (((1,), (1,)), ((), ())),
                                preferred_element_type=F32)
            s = jnp.where(mask, s, -jnp.inf)
            m = jnp.maximum(jnp.max(s, 0, keepdims=True), sink)
            p = jnp.exp(s - m)
            den = jnp.sum(p, 0, keepdims=True) + jnp.exp(sink - m)
            v_t = vv[:, g * HEAD_DIM:(g + 1) * HEAD_DIM].T.astype(BF16)
            o_t = jnp.dot(v_t, p.astype(BF16), preferred_element_type=F32) * (1.0 / den)
            for i, h in enumerate(hs):
                o_ref[qb * BLOCK:(qb + 1) * BLOCK, h * HEAD_DIM:(h + 1) * HEAD_DIM] = (
                    o_t[:, i * BLOCK:(i + 1) * BLOCK].T.astype(BF16))


def _attn_prompt(sinks, q_all, k_all, v_all, l, batch, seq, tq):
    nq = seq // tq
    per = tq // BLOCK
    cur = lambda b, j: (b * nq + j, 0)
    prev = lambda b, j: (jnp.maximum((b * nq + j) * per - 1, 0), 0)
    return pl.pallas_call(
        functools.partial(_attn_prompt_kernel, l=l, tq=tq),
        grid=(batch, nq),
        in_specs=[
            pl.BlockSpec(memory_space=pltpu.SMEM),
            pl.BlockSpec((tq, D_ATTN), cur),
            pl.BlockSpec((BLOCK, D_KV), prev),
            pl.BlockSpec((tq, D_KV), cur),
            pl.BlockSpec((BLOCK, D_KV), prev),
            pl.BlockSpec((tq, D_KV), cur),
        ],
        out_specs=pl.BlockSpec((tq, D_ATTN), cur),
        out_shape=jax.ShapeDtypeStruct((batch * seq, D_ATTN), BF16),
        compiler_params=_params("parallel", "parallel"),
        name="attn_prompt",
    )(sinks, q_all, k_all, k_all, v_all, v_all)


def _attn_sample_kernel(sink_ref, q_ref, kn_ref, vn_ref, kc_ref, vc_ref, o_ref, *, l, steps):
    sb = q_ref.shape[0]
    hist = kc_ref.shape[1]
    rows = GROUP * steps
    pad = jnp.zeros((sb, hist - steps, D_KV), F32)
    kk = jnp.concatenate([kc_ref[...], kn_ref[...], pad], 1).astype(BF16)
    vv = jnp.concatenate([vc_ref[...], vn_ref[...], pad], 1).astype(BF16)
    t = lax.broadcasted_iota(I32, (sb, rows, 2 * hist), 1) % steps
    j = lax.broadcasted_iota(I32, (sb, rows, 2 * hist), 2)
    mask = ((j < hist) & (t + hist - j < WINDOW)) | ((j >= hist) & (j - hist <= t) & (t - (j - hist) < WINDOW))
    head_of_row = lax.broadcasted_iota(I32, (rows, 1), 0) // steps
    dot_qk = lambda a, b: jnp.einsum("bqd,bkd->bqk", a, b, preferred_element_type=F32)
    dot_pv = lambda a, b: jnp.einsum("bqk,bkd->bqd", a, b, preferred_element_type=F32)
    q = q_ref[...]
    for g in range(N_KV_HEADS):
        hs = [g * GROUP + i for i in range(GROUP)]
        q4 = jnp.concatenate([q[:, :, h * HEAD_DIM:(h + 1) * HEAD_DIM] for h in hs], 1)
        sink = jnp.zeros((rows, 1), F32)
        for i, h in enumerate(hs):
            sink = jnp.where(head_of_row == i, sink_ref[l, h], sink)
        s = dot_qk(q4, kk[:, :, g * HEAD_DIM:(g + 1) * HEAD_DIM])
        o = _softmax_sink_pv(s, mask, sink[None], vv[:, :, g * HEAD_DIM:(g + 1) * HEAD_DIM], dot_pv)
        for i, h in enumerate(hs):
            o_ref[:, :, h * HEAD_DIM:(h + 1) * HEAD_DIM] = o[:, i * steps:(i + 1) * steps, :].astype(BF16)


def _attn_sample(sinks, q_s, k_new, v_new, cache_k4, cache_v4, l, sb):
    db, steps, _ = q_s.shape
    hist = cache_k4.shape[2]
    seq3 = lambda i: (i, 0, 0)
    return pl.pallas_call(
        functools.partial(_attn_sample_kernel, l=l, steps=steps),
        grid=(db // sb,),
        in_specs=[
            pl.BlockSpec(memory_space=pltpu.SMEM),
            pl.BlockSpec((sb, steps, D_ATTN), seq3),
            pl.BlockSpec((sb, steps, D_KV), seq3),
            pl.BlockSpec((sb, steps, D_KV), seq3),
            pl.BlockSpec((None, sb, hist, D_KV), lambda i: (l, i, 0, 0)),
            pl.BlockSpec((None, sb, hist, D_KV), lambda i: (l, i, 0, 0)),
        ],
        out_specs=pl.BlockSpec((sb, steps, D_ATTN), seq3),
        out_shape=jax.ShapeDtypeStruct((db, steps, D_ATTN), BF16),
        compiler_params=_params("parallel"),
        name="attn_sample",
    )(sinks, q_s, k_new, v_new, cache_k4, cache_v4)


def _out_proj_kernel(c_ref, a_ref, w_ref, x_ref, o_ref):
    o_ref[...] = (x_ref[...]
                  + jnp.dot(c_ref[...], w_ref[0:D_CONV, :], preferred_element_type=F32)
                  + jnp.dot(a_ref[...], w_ref[D_CONV:, :], preferred_element_type=F32))


def _out_proj(c, a, w_out_bf, x_all, l, row0, tile):
    rows = c.shape[0]
    off = row0 // tile
    return pl.pallas_call(
        _out_proj_kernel,
        grid=(rows // tile,),
        in_specs=[
            pl.BlockSpec((tile, D_CONV), lambda i: (i, 0)),
            pl.BlockSpec((tile, D_ATTN), lambda i: (i, 0)),
            pl.BlockSpec((None, D_MODEL, D_MODEL), lambda i: (l, 0, 0)),
            pl.BlockSpec((tile, D_MODEL), lambda i: (off + i, 0)),
        ],
        out_specs=pl.BlockSpec((tile, D_MODEL), lambda i: (off + i, 0)),
        out_shape=jax.ShapeDtypeStruct(x_all.shape, F32),
        input_output_aliases={3: 0},
        compiler_params=_params("parallel"),
        name="out_proj",
    )(c, a, w_out_bf, x_all)


FF_CHUNK = 256


def _swiglu(x, g_ref, wgu_ref, wd_ref, act_ref):
    h = _rms(x, g_ref[...]).astype(BF16)
    for c0 in range(0, D_FF, FF_CHUNK):
        gate = jnp.dot(h, wgu_ref[:, c0:c0 + FF_CHUNK], preferred_element_type=F32)
        up = jnp.dot(h, wgu_ref[:, D_FF + c0:D_FF + c0 + FF_CHUNK], preferred_element_type=F32)
        act_ref[:, c0:c0 + FF_CHUNK] = (gate * _sigmoid(gate) * up).astype(BF16)
    return jnp.dot(act_ref[...], wd_ref[...], preferred_element_type=F32)


def _ffn_dense_kernel(x_ref, g_ref, wgu_ref, wd_ref, o_ref, act_ref):
    x = x_ref[...]
    o_ref[...] = x + _swiglu(x, g_ref, wgu_ref, wd_ref, act_ref)


def _ffn_dense(x_all, g_ffn3, wgu_bf, wd_bf, l, tile):
    n = x_all.shape[0]
    once = pl.Buffered(1)
    return pl.pallas_call(
        _ffn_dense_kernel,
        grid=(n // tile,),
        in_specs=[
            pl.BlockSpec((tile, D_MODEL), lambda i: (i, 0)),
            pl.BlockSpec((None, 1, D_MODEL), lambda i: (l, 0, 0)),
            pl.BlockSpec((None, D_MODEL, 2 * D_FF), lambda i: (l // 2, 0, 0), pipeline_mode=once),
            pl.BlockSpec((None, D_FF, D_MODEL), lambda i: (l // 2, 0, 0), pipeline_mode=once),
        ],
        out_specs=pl.BlockSpec((tile, D_MODEL), lambda i: (i, 0)),
        out_shape=jax.ShapeDtypeStruct(x_all.shape, F32),
        scratch_shapes=[pltpu.VMEM((tile, D_FF), BF16)],
        input_output_aliases={0: 0},
        compiler_params=_params("parallel"),
        name="ffn_dense",
    )(x_all, g_ffn3, wgu_bf, wd_bf)


INFO_E, INFO_G, INFO_R = 0, 2, 4


def _route_kernel(x_ref, g_ref, whl_ref, info_ref, cnt_ref, run_ref):
    @pl.when(pl.program_id(0) == 0)
    def _():
        run_ref[...] = jnp.zeros_like(run_ref)

    h = _rms(x_ref[...], g_ref[...])
    h_hi = h.astype(BF16)
    h_lo = (h - h_hi.astype(F32)).astype(BF16)
    dot = lambda a, b: jnp.dot(a, b, preferred_element_type=F32)
    hh = dot(h_hi, whl_ref[...])
    logits = hh[:, :LANES] + hh[:, LANES:] + dot(h_lo, whl_ref[:, :LANES])
    t = logits.shape[0]
    lane = lax.broadcasted_iota(I32, logits.shape, 1)
    lg = jnp.where(lane < N_EXPERTS, logits, -jnp.inf)
    m1 = jnp.max(lg, -1, keepdims=True)
    i1 = jnp.min(jnp.where(lg == m1, lane, LANES), -1, keepdims=True)
    lg2 = jnp.where(lane == i1, -jnp.inf, lg)
    m2 = jnp.max(lg2, -1, keepdims=True)
    i2 = jnp.min(jnp.where(lg2 == m2, lane, LANES), -1, keepdims=True)
    e = jnp.exp(m2 - m1)
    g1 = 1.0 / (1.0 + e)
    g2 = e / (1.0 + e)
    sel1 = lane == i1
    sel2 = lane == i2
    onehot = jnp.where(sel1 | sel2, 1.0, 0.0)
    r = lax.broadcasted_iota(I32, (t, t), 0)
    c = lax.broadcasted_iota(I32, (t, t), 1)
    tri = jnp.where(r > c, 1.0, 0.0).astype(BF16)
    before = dot(tri, onehot.astype(BF16)) + run_ref[0:1, :]
    r1 = jnp.sum(jnp.where(sel1, before, 0.0), -1, keepdims=True)
    r2 = jnp.sum(jnp.where(sel2, before, 0.0), -1, keepdims=True)
    run_ref[...] = run_ref[...] + jnp.sum(onehot, 0, keepdims=True)
    cnt_ref[...] = run_ref[...]
    info = jnp.zeros(logits.shape, F32)
    for pos, val in ((INFO_E, i1.astype(F32)), (INFO_E + 1, i2.astype(F32)), (INFO_G, g1), (INFO_G + 1, g2),
                     (INFO_R, r1), (INFO_R + 1, r2)):
        info = jnp.where(lane == pos, val, info)
    info_ref[...] = info


def _route(x_all, g_ffn3, wr_hl, l, tile):
    n = x_all.shape[0]
    return pl.pallas_call(
        _route_kernel,
        grid=(n // tile,),
        in_specs=[
            pl.BlockSpec((tile, D_MODEL), lambda i: (i, 0)),
            pl.BlockSpec((None, 1, D_MODEL), lambda i: (l, 0, 0)),
            pl.BlockSpec((None, D_MODEL, 2 * LANES), lambda i: (l // 2, 0, 0)),
        ],
        out_specs=[
            pl.BlockSpec((tile, LANES), lambda i: (i, 0)),
            pl.BlockSpec((SUBLANES, LANES), lambda i: (0, 0)),
        ],
        out_shape=[
            jax.ShapeDtypeStruct((n, LANES), F32),
            jax.ShapeDtypeStruct((SUBLANES, LANES), F32),
        ],
        scratch_shapes=[pltpu.VMEM((SUBLANES, LANES), F32)],
        compiler_params=_params("arbitrary"),
        name="route",
    )(x_all, g_ffn3, wr_hl)


SLAB = D_MODEL // LANES
assert SLAB == SUBLANES


def _slab(ref, row, rows=1):
    start = row * SLAB
    if not isinstance(start, int):
        start = pl.multiple_of(start, SLAB)
    return ref.at[pl.ds(start, rows * SLAB)]


def _to_slabs(slab_ref, x):
    for s in range(SLAB):
        slab_ref[pl.ds(s, x.shape[0], stride=SLAB), :] = x[:, s * LANES:(s + 1) * LANES]


def _from_slabs(slab_ref, rows):
    return jnp.concatenate([slab_ref[pl.ds(s, rows, stride=SLAB), :] for s in range(SLAB)], 1)


def _start_row_copies(tile, make_copy):
    def body(group, carry):
        for row in range(SUBLANES):
            for j in range(2):
                make_copy(group * SUBLANES + row, j).start(priority=row % 2)
        return carry

    lax.fori_loop(0, tile // SUBLANES, body, 0)


CAST_PARTS = 8


def _dispatch_kernel(pad_start_ref, pad_len_ref, pos_ref, x_ref, wgu_ref, wd_ref, xs_ref, wgu_bf_ref, wd_bf_ref,
                     rows_ref, zero_ref, sem, zsem, *, tile, tm):
    first = pl.program_id(0) == 0
    wgu_bf_ref[...] = wgu_ref[...].astype(BF16)
    wd_bf_ref[...] = wd_ref[...].astype(BF16)

    def pad_copies(action):
        for e in range(N_EXPERTS):
            start = pad_start_ref[e]
            run = tm // 2
            while run >= 1:
                take = (pad_len_ref[e] & run) != 0

                @pl.when(take)
                def _():
                    action(pltpu.make_async_copy(_slab(zero_ref, 0, run), _slab(xs_ref, start, run), zsem))

                start = start + jnp.where(take, run, 0)
                run //= 2
        n_tiles = xs_ref.shape[0] // (tm * SLAB)
        for t in range(n_tiles - (N_EXPERTS - 1), n_tiles):
            @pl.when(t >= pad_len_ref[N_EXPERTS])
            def _():
                for half in range(2):
                    action(pltpu.make_async_copy(zero_ref, _slab(xs_ref, t * tm + half * (tm // 2), tm // 2), zsem))

    @pl.when(first)
    def _():
        zero_ref[...] = jnp.zeros_like(zero_ref)
        pad_copies(lambda c: c.start())

    _to_slabs(rows_ref, x_ref[...])
    _start_row_copies(tile, lambda i, j: pltpu.make_async_copy(
        _slab(rows_ref, i), _slab(xs_ref, pos_ref[0, j, i]), sem.at[j]))
    for j in range(2):
        pltpu.make_async_copy(_slab(xs_ref, 0, tile), _slab(xs_ref, 0, tile), sem.at[j]).wait()

    @pl.when(first)
    def _():
        pad_copies(lambda c: c.wait())


def _dispatch(pad_start, pad_len, pos3, x_all, w_gu, w_down, l, n_rows, tile, tm):
    n = x_all.shape[0]
    steps = n // tile
    assert steps >= N_EXPERTS * CAST_PARTS
    m = l // 2
    gu_rows, d_rows = D_MODEL // CAST_PARTS, D_FF // CAST_PARTS

    def part(i):
        c = jnp.minimum(i, N_EXPERTS * CAST_PARTS - 1)
        return c // CAST_PARTS, c % CAST_PARTS

    grid_spec = pltpu.PrefetchScalarGridSpec(
        num_scalar_prefetch=2,
        grid=(steps,),
        in_specs=[
            pl.BlockSpec((1, 2, tile), lambda i, ps, pn: (i, 0, 0), memory_space=pltpu.SMEM),
            pl.BlockSpec((tile, D_MODEL), lambda i, ps, pn: (i, 0)),
            pl.BlockSpec((None, None, gu_rows, 2 * D_FF), lambda i, ps, pn: (m, *part(i), 0)),
            pl.BlockSpec((None, None, d_rows, D_MODEL), lambda i, ps, pn: (m, *part(i), 0)),
        ],
        out_specs=[
            pl.BlockSpec(memory_space=pl.ANY),
            pl.BlockSpec((None, gu_rows, 2 * D_FF), lambda i, ps, pn: (*part(i), 0)),
            pl.BlockSpec((None, d_rows, D_MODEL), lambda i, ps, pn: (*part(i), 0)),
        ],
        scratch_shapes=[pltpu.VMEM((tile * SLAB, LANES), F32), pltpu.VMEM((tm // 2 * SLAB, LANES), F32),
                        pltpu.SemaphoreType.DMA((2,)), pltpu.SemaphoreType.DMA(())],
    )
    return pl.pallas_call(
        functools.partial(_dispatch_kernel, tile=tile, tm=tm),
        grid_spec=grid_spec,
        out_shape=[
            jax.ShapeDtypeStruct((n_rows * SLAB, LANES), F32),
            jax.ShapeDtypeStruct((N_EXPERTS, D_MODEL, 2 * D_FF), BF16),
            jax.ShapeDtypeStruct((N_EXPERTS, D_FF, D_MODEL), BF16),
        ],
        compiler_params=_params("arbitrary"),
        name="moe_dispatch",
    )(pad_start, pad_len, pos3, x_all, w_gu, w_down)


def _moe_kernel(te_ref, used_ref, xs_ref, g_ref, wgu_ref, wd_ref, y_ref, act_ref, *, tm):
    used = pl.program_id(0) < used_ref[0]

    @pl.when(used)
    def _():
        _to_slabs(y_ref, _swiglu(_from_slabs(xs_ref, tm), g_ref, wgu_ref, wd_ref, act_ref))

    @pl.when(jnp.logical_not(used))
    def _():
        y_ref[...] = jnp.zeros_like(y_ref)


def _moe(tile_expert, n_used, xs, g_ffn3, wgu_bf, wd_bf, l, tm):
    nt = xs.shape[0] // (tm * SLAB)

    def tile_idx(t, used):
        return jnp.minimum(t, used[0] - 1)

    grid_spec = pltpu.PrefetchScalarGridSpec(
        num_scalar_prefetch=2,
        grid=(nt,),
        in_specs=[
            pl.BlockSpec((tm * SLAB, LANES), lambda t, te, used: (tile_idx(t, used), 0)),
            pl.BlockSpec((None, 1, D_MODEL), lambda t, te, used: (l, 0, 0)),
            pl.BlockSpec((None, D_MODEL, 2 * D_FF), lambda t, te, used: (te[tile_idx(t, used)], 0, 0)),
            pl.BlockSpec((None, D_FF, D_MODEL), lambda t, te, used: (te[tile_idx(t, used)], 0, 0)),
        ],
        out_specs=pl.BlockSpec((tm * SLAB, LANES), lambda t, te, used: (t, 0)),
        scratch_shapes=[pltpu.VMEM((tm, D_FF), BF16)],
    )
    return pl.pallas_call(
        functools.partial(_moe_kernel, tm=tm),
        grid_spec=grid_spec,
        out_shape=jax.ShapeDtypeStruct(xs.shape, F32),
        compiler_params=_params("arbitrary"),
        name="moe_experts",
    )(tile_expert, n_used, xs, g_ffn3, wgu_bf, wd_bf)


def _gather_combine(pos_ref, info_ref, x_ref, y_ref, buf_ref, sem, tile):
    _start_row_copies(tile, lambda i, j: pltpu.make_async_copy(
        _slab(y_ref, pos_ref[0, j, i]), _slab(buf_ref.at[j], i), sem.at[j]))
    for j in range(2):
        pltpu.make_async_copy(_slab(y_ref, 0, tile), _slab(y_ref, 0, tile), sem.at[j]).wait()
    info = info_ref[...]
    return (x_ref[...]
            + info[:, INFO_G:INFO_G + 1] * _from_slabs(buf_ref.at[0], tile)
            + info[:, INFO_G + 1:INFO_G + 2] * _from_slabs(buf_ref.at[1], tile))


def _combine_kernel(pos_ref, info_ref, x_ref, y_ref, o_ref, buf_ref, sem, *, tile):
    o_ref[...] = _gather_combine(pos_ref, info_ref, x_ref, y_ref, buf_ref, sem, tile)


def _combine_final_kernel(pos_ref, info_ref, x_ref, y_ref, g_ref, op_ref, os_ref, buf_ref, sem, *, tile, steps_p):
    out = _rms(_gather_combine(pos_ref, info_ref, x_ref, y_ref, buf_ref, sem, tile), g_ref[...])
    is_prompt = pl.program_id(0) < steps_p

    @pl.when(is_prompt)
    def _():
        op_ref[...] = out

    @pl.when(jnp.logical_not(is_prompt))
    def _():
        os_ref[...] = out


def _combine(pos3, info, x_all, y, tile, final=None):
    n = x_all.shape[0]
    row = lambda i: (i, 0)
    in_specs = [
        pl.BlockSpec((1, 2, tile), lambda i: (i, 0, 0), memory_space=pltpu.SMEM),
        pl.BlockSpec((tile, LANES), row),
        pl.BlockSpec((tile, D_MODEL), row),
        pl.BlockSpec(memory_space=pl.ANY),
    ]
    scratch = [pltpu.VMEM((2, tile * SLAB, LANES), F32), pltpu.SemaphoreType.DMA((2,))]
    if final is None:
        return pl.pallas_call(
            functools.partial(_combine_kernel, tile=tile),
            grid=(n // tile,),
            in_specs=in_specs,
            out_specs=pl.BlockSpec((tile, D_MODEL), row),
            out_shape=jax.ShapeDtypeStruct(x_all.shape, F32),
            scratch_shapes=scratch,
            input_output_aliases={2: 0},
            compiler_params=_params("arbitrary"),
            name="moe_combine",
        )(pos3, info, x_all, y)
    g_fin, n_p = final
    steps_p = n_p // tile
    return pl.pallas_call(
        functools.partial(_combine_final_kernel, tile=tile, steps_p=steps_p),
        grid=(n // tile,),
        in_specs=in_specs + [pl.BlockSpec((1, D_MODEL), lambda i: (0, 0))],
        out_specs=[
            pl.BlockSpec((tile, D_MODEL), lambda i: (jnp.minimum(i, steps_p - 1), 0)),
            pl.BlockSpec((tile, D_MODEL), lambda i: (jnp.maximum(i - steps_p, 0), 0)),
        ],
        out_shape=[jax.ShapeDtypeStruct((n_p, D_MODEL), F32), jax.ShapeDtypeStruct((n - n_p, D_MODEL), F32)],
        scratch_shapes=scratch,
        compiler_params=_params("arbitrary"),
        name="moe_combine_final",
    )(pos3, info, x_all, y, g_fin)


def _ffn_moe(x_all, g_ffn3, wr_hl, w_moe_gu, w_moe_down, l, tile, tm, final=None):
    n = x_all.shape[0]
    td = next(t for t in (256, 128, 64, 32, 16, 8) if n % t == 0 and n // t >= N_EXPERTS * CAST_PARTS)
    info, cnt = _route(x_all, g_ffn3, wr_hl, l, tile)
    counts = cnt[0, :N_EXPERTS].astype(I32)
    tiles_per = (counts + tm - 1) // tm
    tile_end = jnp.cumsum(tiles_per)
    base = (tile_end - tiles_per) * tm
    experts = info[:, INFO_E:INFO_E + 2].astype(I32)
    ranks = info[:, INFO_R:INFO_R + 2].astype(I32)
    pos = jnp.sum(jnp.where(experts[..., None] == jnp.arange(N_EXPERTS), base, 0), -1) + ranks
    n_tiles = (2 * n + N_EXPERTS * (tm - 1)) // tm
    n_used = tile_end[-1:]
    tile_expert = jnp.minimum(jnp.sum(jnp.arange(n_tiles)[:, None] >= tile_end[None, :], -1), N_EXPERTS - 1)
    blocked = lambda t: pos.T.reshape(2, n // t, t).transpose(1, 0, 2)
    pad_len = jnp.concatenate([tiles_per * tm - counts, n_used])
    xs, wgu_bf, wd_bf = _dispatch(base + counts, pad_len, blocked(td), x_all, w_moe_gu, w_moe_down, l,
                                  n_tiles * tm, td, tm)
    y = _moe(tile_expert.astype(I32), n_used.astype(I32), xs, g_ffn3, wgu_bf, wd_bf, l, tm)
    return _combine(blocked(tile), info, x_all, y, tile, final)


def _final_kernel(x_ref, g_ref, o_ref):
    o_ref[...] = _rms(x_ref[...], g_ref[...])


def _final_norm(x_all, g2, row0, rows, tile):
    off = row0 // tile
    return pl.pallas_call(
        _final_kernel,
        grid=(rows // tile,),
        in_specs=[
            pl.BlockSpec((tile, D_MODEL), lambda i: (off + i, 0)),
            pl.BlockSpec((1, D_MODEL), lambda i: (0, 0)),
        ],
        out_specs=pl.BlockSpec((tile, D_MODEL), lambda i: (i, 0)),
        out_shape=jax.ShapeDtypeStruct((rows, D_MODEL), F32),
        compiler_params=_params("parallel"),
        name="final_norm",
    )(x_all, g2)


def _rope_tables(pos):
    half = HEAD_DIM // 2
    inv = ROPE_THETA ** (-jnp.arange(half, dtype=F32) / half)
    ang = pos.astype(F32)[:, None] * inv[None, :]
    cos = jnp.cos(ang)
    sin = jnp.sin(ang)
    reps = LANES // HEAD_DIM
    return jnp.tile(jnp.concatenate([cos, cos], -1), (1, reps)), jnp.tile(jnp.concatenate([-sin, sin], -1), (1, reps))


def kernel(x_prompt, x_sample, cache_conv, cache_swa_k, cache_swa_v, g_mix, w_in, conv_w, conv_b, conv_ln_g,
           conv_ln_b, attn_sinks, w_out, g_ffn, w_dense_gu, w_dense_down, w_router, w_moe_gu, w_moe_down, g_final):
    batch, seq, _ = x_prompt.shape
    db, steps, _ = x_sample.shape
    depth = g_mix.shape[0]
    hist = cache_swa_k.shape[2]
    n_p, n_s = batch * seq, db * steps
    n = n_p + n_s
    assert seq % BLOCK == 0 and seq >= CONV_WIDTH - 1 and steps <= hist and steps < CONV_WIDTH - 1
    assert hist == min(WINDOW, PAST_LEN)
    tile = _pick_tile((1024, 512, 256, 128), n_p, n_s)
    tq = _pick_tile((512, 128), seq)
    tm = _pick_tile((512, 256), 2 * n)
    sb_attn = _pick_tile((16, 8), db)
    assert db % SUBLANES == 0
    conv_wb = jnp.broadcast_to(conv_w[:, :, None, :], (depth, CONV_WIDTH, SUBLANES, D_CONV))

    x_all = jnp.concatenate([x_prompt.reshape(n_p, D_MODEL), x_sample.reshape(n_s, D_MODEL)], 0)
    pos_all = jnp.concatenate([jnp.tile(jnp.arange(seq, dtype=I32), batch),
                               jnp.tile(PAST_LEN + jnp.arange(steps, dtype=I32), db)])
    cos_t, sin_t = _rope_tables(pos_all)

    vec3 = lambda a: a.reshape(a.shape[0], 1, a.shape[1])
    g_mix3, g_ffn3, conv_b3, ln_g3, ln_b3 = map(vec3, (g_mix, g_ffn, conv_b, conv_ln_g, conv_ln_b))
    w_in_bf, w_out_bf = w_in.astype(BF16), w_out.astype(BF16)
    wgu_bf, wd_bf = w_dense_gu.astype(BF16), w_dense_down.astype(BF16)
    wr = jnp.pad(w_router, ((0, 0), (0, 0), (0, LANES - N_EXPERTS)))
    wr_hi = wr.astype(BF16)
    wr_hl = jnp.concatenate([wr_hi, (wr - wr_hi.astype(F32)).astype(BF16)], -1)
    cache_k4 = cache_swa_k.reshape(depth, db, hist, D_KV)
    cache_v4 = cache_swa_v.reshape(depth, db, hist, D_KV)

    def tail_rows(a, rows):
        return jnp.stack([a[(b + 1) * seq - rows:(b + 1) * seq] for b in range(batch)], 0)

    keep = min(WINDOW, seq)
    g_fin = g_final.reshape(1, D_MODEL)
    conv_p, k_p, v_p, u_new, k_new, v_new = [], [], [], [], [], []
    for l in range(depth):
        u, q, k, v = _in_proj(x_all, g_mix3, w_in_bf, cos_t, sin_t, l, tile)

        c_p = _conv_prompt(u, conv_wb, conv_b3, ln_g3, ln_b3, l, batch, seq, tq)
        a_p = _attn_prompt(attn_sinks, q, k, v, l, batch, seq, tq)
        x_all = _out_proj(c_p, a_p, w_out_bf, x_all, l, 0, tile)

        u_s = u[n_p:].reshape(db, steps, D_CONV)
        k_s = k[n_p:].reshape(db, steps, D_KV)
        v_s = v[n_p:].reshape(db, steps, D_KV)
        c_s = _conv_sample(cache_conv, u_s, conv_wb, conv_b3, ln_g3, ln_b3, l).reshape(n_s, D_CONV)
        a_s = _attn_sample(attn_sinks, q[n_p:].reshape(db, steps, D_ATTN), k_s, v_s, cache_k4, cache_v4,
                           l, sb_attn).reshape(n_s, D_ATTN)
        x_all = _out_proj(c_s, a_s, w_out_bf, x_all, l, n_p, tile)

        conv_p.append(tail_rows(u, CONV_WIDTH - 1))
        k_p.append(tail_rows(k, keep).reshape(batch, keep, N_KV_HEADS, HEAD_DIM))
        v_p.append(tail_rows(v, keep).reshape(batch, keep, N_KV_HEADS, HEAD_DIM))
        u_new.append(u_s)
        k_new.append(k_s.reshape(db, steps, N_KV_HEADS, HEAD_DIM))
        v_new.append(v_s.reshape(db, steps, N_KV_HEADS, HEAD_DIM))

        if l % 2 == 0:
            x_all = _ffn_dense(x_all, g_ffn3, wgu_bf, wd_bf, l, tile)
        elif l < depth - 1:
            x_all = _ffn_moe(x_all, g_ffn3, wr_hl, w_moe_gu, w_moe_down, l, tile, tm)
        else:
            y_p, y_s = _ffn_moe(x_all, g_ffn3, wr_hl, w_moe_gu, w_moe_down, l, tile, tm, final=(g_fin, n_p))

    if depth % 2:
        y_p, y_s = _final_norm(x_all, g_fin, 0, n_p, tile), _final_norm(x_all, g_fin, n_p, n_s, tile)
    y_prompt = y_p.reshape(batch, seq, D_MODEL)
    y_sample = y_s.reshape(db, steps, D_MODEL)
    state_s = lambda cache, new: jnp.concatenate([cache[:, :, steps:], jnp.stack(new, 0)], 2)
    return (y_prompt, y_sample, jnp.stack(conv_p, 0), jnp.stack(k_p, 0), jnp.stack(v_p, 0),
            state_s(cache_conv, u_new), state_s(cache_swa_k, k_new), state_s(cache_swa_v, v_new))
```

```python
import functools

import jax
import jax.numpy as jnp
from jax import lax
from jax.experimental import pallas as pl
from jax.experimental.pallas import tpu as pltpu

F32 = jnp.float32
BF16 = jnp.bfloat16
I32 = jnp.int32

D_MODEL = 1024
D_CONV = 512
N_HEADS = 8
HEAD_DIM = 64
N_KV_HEADS = 2
GROUP = N_HEADS // N_KV_HEADS
D_ATTN = N_HEADS * HEAD_DIM
D_KV = N_KV_HEADS * HEAD_DIM
D_IN = 2 * D_CONV + D_ATTN + 2 * D_KV
CONV_WIDTH = 31
WINDOW = 128
BLOCK = 128
ROPE_THETA = 10000.0
D_FF = 2816
N_EXPERTS = 8
EPS = 1e-6
PAST_LEN = 8192

LANES = 128
SUBLANES = 8
CONV_HALO = 32
CONV_PAD = CONV_HALO - (CONV_WIDTH - 1)
VMEM_LIMIT = 56 * 1024 * 1024


def _pick_tile(cands, *sizes):
    for c in cands:
        if all(s % c == 0 for s in sizes):
            return c
    raise ValueError(f"no tile in {cands} divides {sizes}")


def _params(*sem):
    return pltpu.CompilerParams(dimension_semantics=sem, vmem_limit_bytes=VMEM_LIMIT)


def _rms(x, g):
    return x * lax.rsqrt(jnp.mean(x * x, -1, keepdims=True) + EPS) * g


def _sigmoid(x):
    return 1.0 / (1.0 + jnp.exp(-x))


def _in_proj_kernel(x_ref, g_ref, w_ref, cos_ref, sin_ref, u_ref, q_ref, k_ref, v_ref):
    h = _rms(x_ref[...], g_ref[...]).astype(BF16)
    p = jnp.dot(h, w_ref[...], preferred_element_type=F32)
    u_ref[...] = p[:, :D_CONV] * _sigmoid(p[:, D_CONV:2 * D_CONV])
    cos = cos_ref[...]
    sin = sin_ref[...]
    lane = lax.broadcasted_iota(I32, cos.shape, 1)
    first_half = (lane % HEAD_DIM) < (HEAD_DIM // 2)

    def rope(xc):
        partner = jnp.where(first_half,
                            pltpu.roll(xc, LANES - HEAD_DIM // 2, 1),
                            pltpu.roll(xc, HEAD_DIM // 2, 1))
        return xc * cos + partner * sin

    q0 = 2 * D_CONV
    for c in range(D_ATTN // LANES):
        qc = rope(p[:, q0 + c * LANES:q0 + (c + 1) * LANES])
        q_ref[:, c * LANES:(c + 1) * LANES] = (qc * (HEAD_DIM ** -0.5)).astype(BF16)
    k0 = q0 + D_ATTN
    k_ref[...] = rope(p[:, k0:k0 + D_KV])
    v_ref[...] = p[:, k0 + D_KV:k0 + 2 * D_KV]


def _in_proj(x_all, g_mix3, w_in_bf, cos_t, sin_t, l, tile):
    n = x_all.shape[0]
    row = lambda i: (i, 0)
    return pl.pallas_call(
        _in_proj_kernel,
        grid=(n // tile,),
        in_specs=[
            pl.BlockSpec((tile, D_MODEL), row),
            pl.BlockSpec((None, 1, D_MODEL), lambda i: (l, 0, 0)),
            pl.BlockSpec((None, D_MODEL, D_IN), lambda i: (l, 0, 0)),
            pl.BlockSpec((tile, LANES), row),
            pl.BlockSpec((tile, LANES), row),
        ],
        out_specs=[
            pl.BlockSpec((tile, D_CONV), row),
            pl.BlockSpec((tile, D_ATTN), row),
            pl.BlockSpec((tile, D_KV), row),
            pl.BlockSpec((tile, D_KV), row),
        ],
        out_shape=[
            jax.ShapeDtypeStruct((n, D_CONV), F32),
            jax.ShapeDtypeStruct((n, D_ATTN), BF16),
            jax.ShapeDtypeStruct((n, D_KV), F32),
            jax.ShapeDtypeStruct((n, D_KV), F32),
        ],
        compiler_params=_params("parallel"),
        name="in_proj",
    )(x_all, g_mix3, w_in_bf, cos_t, sin_t)


def _ln_swish(y, g, b):
    mu = jnp.mean(y, -1, keepdims=True)
    yc = y - mu
    z = yc * lax.rsqrt(jnp.mean(yc * yc, -1, keepdims=True) + EPS) * g + b
    return z * _sigmoid(z)


def _conv_prompt_kernel(u_ref, wb_ref, cb_ref, lg_ref, lb_ref, c_ref, ext_ref, sh_ref, y_ref, *, tile, rb):
    @pl.when(pl.program_id(1) == 0)
    def _():
        ext_ref[0:CONV_HALO, :] = jnp.zeros((CONV_HALO, D_CONV), F32)

    ext_ref[CONV_HALO:CONV_HALO + tile, :] = u_ref[...]
    for b in range(1, SUBLANES):
        sh_ref[b - 1] = ext_ref[pl.ds(b, sh_ref.shape[1]), :]
    def row_block(i, carry):
        r0 = pl.multiple_of(i * rb, rb)
        accs = [jnp.broadcast_to(cb_ref[...], (SUBLANES, D_CONV))] * (rb // SUBLANES)
        for k in range(CONV_WIDTH):
            b = (CONV_PAD + k) % SUBLANES
            wk = wb_ref[k]
            for rg in range(rb // SUBLANES):
                row = pl.multiple_of(r0 + (CONV_PAD + k - b) + rg * SUBLANES, SUBLANES)
                src = ext_ref[pl.ds(row, SUBLANES), :] if b == 0 else sh_ref[b - 1, pl.ds(row, SUBLANES), :]
                accs[rg] = accs[rg] + src * wk
        for rg in range(rb // SUBLANES):
            y_ref[pl.ds(pl.multiple_of(r0 + rg * SUBLANES, SUBLANES), SUBLANES), :] = accs[rg]
        return carry

    lax.fori_loop(0, tile // rb, row_block, 0)
    c_ref[...] = _ln_swish(y_ref[...], lg_ref[...], lb_ref[...]).astype(BF16)
    ext_ref[0:CONV_HALO, :] = ext_ref[tile:tile + CONV_HALO, :]


def _conv_prompt(u_all, conv_wb, conv_b3, ln_g3, ln_b3, l, batch, seq, tile):
    nt = seq // tile
    vec = pl.BlockSpec((None, 1, D_CONV), lambda b, j: (l, 0, 0))
    return pl.pallas_call(
        functools.partial(_conv_prompt_kernel, tile=tile, rb=32),
        grid=(batch, nt),
        in_specs=[
            pl.BlockSpec((tile, D_CONV), lambda b, j: (b * nt + j, 0)),
            pl.BlockSpec((None, CONV_WIDTH, SUBLANES, D_CONV), lambda b, j: (l, 0, 0, 0)),
            vec, vec, vec,
        ],
        out_specs=pl.BlockSpec((tile, D_CONV), lambda b, j: (b * nt + j, 0)),
        out_shape=jax.ShapeDtypeStruct((batch * seq, D_CONV), BF16),
        scratch_shapes=[pltpu.VMEM((tile + CONV_HALO, D_CONV), F32),
                        pltpu.VMEM((SUBLANES - 1, tile + CONV_HALO - SUBLANES, D_CONV), F32),
                        pltpu.VMEM((tile, D_CONV), F32)],
        compiler_params=_params("arbitrary", "arbitrary"),
        name="conv_prompt",
    )(u_all, conv_wb, conv_b3, ln_g3, ln_b3)


def _conv_sample_kernel(hist_ref, u_ref, wb_ref, cb_ref, lg_ref, lb_ref, c_ref):
    sb, steps, _ = u_ref.shape
    hist = CONV_WIDTH - 1
    accs = [jnp.broadcast_to(cb_ref[...], (sb, D_CONV)) for _ in range(steps)]
    for r in range(hist + steps):
        row = hist_ref[:, r, :] if r < hist else u_ref[:, r - hist, :]
        for t in range(steps):
            if 0 <= r - t < CONV_WIDTH:
                accs[t] = accs[t] + row * wb_ref[r - t]
    for t in range(steps):
        c_ref[:, t, :] = _ln_swish(accs[t], lg_ref[...], lb_ref[...]).astype(BF16)


def _conv_sample(cache_conv, u_s, conv_wb, conv_b3, ln_g3, ln_b3, l):
    db, steps, _ = u_s.shape
    sb = SUBLANES
    vec = pl.BlockSpec((None, 1, D_CONV), lambda i: (l, 0, 0))
    return pl.pallas_call(
        _conv_sample_kernel,
        grid=(db // sb,),
        in_specs=[
            pl.BlockSpec((None, sb, CONV_WIDTH - 1, D_CONV), lambda i: (l, i, 0, 0)),
            pl.BlockSpec((sb, steps, D_CONV), lambda i: (i, 0, 0)),
            pl.BlockSpec((None, CONV_WIDTH, SUBLANES, D_CONV), lambda i: (l, 0, 0, 0)),
            vec, vec, vec,
        ],
        out_specs=pl.BlockSpec((sb, steps, D_CONV), lambda i: (i, 0, 0)),
        out_shape=jax.ShapeDtypeStruct((db, steps, D_CONV), BF16),
        compiler_params=_params("parallel"),
        name="conv_sample",
    )(cache_conv, u_s, conv_wb, conv_b3, ln_g3, ln_b3)


def _softmax_sink_pv(s, mask, sink, v_bf, dot_pv):
    s = jnp.where(mask, s, -jnp.inf)
    m = jnp.maximum(jnp.max(s, -1, keepdims=True), sink)
    p = jnp.exp(s - m)
    den = jnp.sum(p, -1, keepdims=True) + jnp.exp(sink - m)
    return dot_pv(p.astype(BF16), v_bf) / den


def _attn_prompt_kernel(sink_ref, q_ref, kp_ref, kc_ref, vp_ref, vc_ref, c_ref, w_ref, x_ref, xo_ref, o_ref,
                        *, l, tq):
    first = pl.program_id(1) == 0
    cols = GROUP * BLOCK
    kj = lax.broadcasted_iota(I32, (2 * BLOCK, cols), 0)
    qi = lax.broadcasted_iota(I32, (2 * BLOCK, cols), 1) % BLOCK
    band = (kj > qi) & (kj <= qi + BLOCK)
    head_of_col = lax.broadcasted_iota(I32, (1, cols), 1) // BLOCK
    for qb in range(tq // BLOCK):
        if qb == 0:
            kk = jnp.concatenate([kp_ref[...], kc_ref[0:BLOCK, :]], 0)
            vv = jnp.concatenate([vp_ref[...], vc_ref[0:BLOCK, :]], 0)
            mask = band & (kj >= jnp.where(first, BLOCK, 0))
        else:
            kk = kc_ref[(qb - 1) * BLOCK:(qb + 1) * BLOCK, :]
            vv = vc_ref[(qb - 1) * BLOCK:(qb + 1) * BLOCK, :]
            mask = band
        kk = kk.astype(BF16)
        q = q_ref[qb * BLOCK:(qb + 1) * BLOCK, :]
        for g in range(N_KV_HEADS):
            hs = [g * GROUP + i for i in range(GROUP)]
            q4 = jnp.concatenate([q[:, h * HEAD_DIM:(h + 1) * HEAD_DIM] for h in hs], 0)
            sink = jnp.zeros((1, cols), F32)
            for i, h in enumerate(hs):
                sink = jnp.where(head_of_col == i, sink_ref[l, h], sink)
            s = lax.dot_general(kk[:, g * HEAD_DIM:(g + 1) * HEAD_DIM], q4, (((1,), (1,)), ((), ())),
                                preferred_element_type=F32)
            s = jnp.where(mask, s, -jnp.inf)
            m = jnp.maximum(jnp.max(s, 0, keepdims=True), sink)
            p = jnp.exp(s - m)
            den = jnp.sum(p, 0, keepdims=True) + jnp.exp(sink - m)
            v_t = vv[:, g * HEAD_DIM:(g + 1) * HEAD_DIM].T.astype(BF16)
            o_t = jnp.dot(v_t, p.astype(BF16), preferred_element_type=F32) * (1.0 / den)
            for i, h in enumerate(hs):
                o_ref[qb * BLOCK:(qb + 1) * BLOCK, h * HEAD_DIM:(h + 1) * HEAD_DIM] = (
                    o_t[:, i * BLOCK:(i + 1) * BLOCK].T.astype(BF16))
        rows = slice(qb * BLOCK, (qb + 1) * BLOCK)
        xo_ref[rows, :] = (x_ref[rows, :]
                           + jnp.dot(c_ref[rows, :], w_ref[0:D_CONV, :], preferred_element_type=F32)
                           + jnp.dot(o_ref[rows, :], w_ref[D_CONV:, :], preferred_element_type=F32))


def _attn_out_prompt(sinks, q_all, k_all, v_all, c_p, w_out_bf, x_all, l, batch, seq, tq):
    nq = seq // tq
    per = tq // BLOCK
    cur = lambda b, j: (b * nq + j, 0)
    prev = lambda b, j: (jnp.maximum((b * nq + j) * per - 1, 0), 0)
    return pl.pallas_call(
        functools.partial(_attn_prompt_kernel, l=l, tq=tq),
        grid=(batch, nq),
        in_specs=[
            pl.BlockSpec(memory_space=pltpu.SMEM),
            pl.BlockSpec((tq, D_ATTN), cur),
            pl.BlockSpec((BLOCK, D_KV), prev),
            pl.BlockSpec((tq, D_KV), cur),
            pl.BlockSpec((BLOCK, D_KV), prev),
            pl.BlockSpec((tq, D_KV), cur),
            pl.BlockSpec((tq, D_CONV), cur),
            pl.BlockSpec((None, D_MODEL, D_MODEL), lambda b, j: (l, 0, 0)),
            pl.BlockSpec((tq, D_MODEL), cur),
        ],
        out_specs=pl.BlockSpec((tq, D_MODEL), cur),
        out_shape=jax.ShapeDtypeStruct(x_all.shape, F32),
        scratch_shapes=[pltpu.VMEM((tq, D_ATTN), BF16)],
        input_output_aliases={8: 0},
        compiler_params=_params("parallel", "parallel"),
        name="attn_out_prompt",
    )(sinks, q_all, k_all, k_all, v_all, v_all, c_p, w_out_bf, x_all)


def _attn_sample_kernel(sink_ref, q_ref, kn_ref, vn_ref, kc_ref, vc_ref, o_ref, *, l, steps):
    sb = q_ref.shape[0]
    hist = kc_ref.shape[1]
    rows = GROUP * steps
    pad = jnp.zeros((sb, hist - steps, D_KV), F32)
    kk = jnp.concatenate([kc_ref[...], kn_ref[...], pad], 1).astype(BF16)
    vv = jnp.concatenate([vc_ref[...], vn_ref[...], pad], 1).astype(BF16)
    t = lax.broadcasted_iota(I32, (sb, rows, 2 * hist), 1) % steps
    j = lax.broadcasted_iota(I32, (sb, rows, 2 * hist), 2)
    mask = ((j < hist) & (t + hist - j < WINDOW)) | ((j >= hist) & (j - hist <= t) & (t - (j - hist) < WINDOW))
    head_of_row = lax.broadcasted_iota(I32, (rows, 1), 0) // steps
    dot_qk = lambda a, b: jnp.einsum("bqd,bkd->bqk", a, b, preferred_element_type=F32)
    dot_pv = lambda a, b: jnp.einsum("bqk,bkd->bqd", a, b, preferred_element_type=F32)
    q = q_ref[...]
    for g in range(N_KV_HEADS):
        hs = [g * GROUP + i for i in range(GROUP)]
        q4 = jnp.concatenate([q[:, :, h * HEAD_DIM:(h + 1) * HEAD_DIM] for h in hs], 1)
        sink = jnp.zeros((rows, 1), F32)
        for i, h in enumerate(hs):
            sink = jnp.where(head_of_row == i, sink_ref[l, h], sink)
        s = dot_qk(q4, kk[:, :, g * HEAD_DIM:(g + 1) * HEAD_DIM])
        o = _softmax_sink_pv(s, mask, sink[None], vv[:, :, g * HEAD_DIM:(g + 1) * HEAD_DIM], dot_pv)
        for i, h in enumerate(hs):
            o_ref[:, :, h * HEAD_DIM:(h + 1) * HEAD_DIM] = o[:, i * steps:(i + 1) * steps, :].astype(BF16)


def _attn_sample(sinks, q_s, k_new, v_new, cache_k4, cache_v4, l, sb):
    db, steps, _ = q_s.shape
    hist = cache_k4.shape[2]
    seq3 = lambda i: (i, 0, 0)
    return pl.pallas_call(
        functools.partial(_attn_sample_kernel, l=l, steps=steps),
        grid=(db // sb,),
        in_specs=[
            pl.BlockSpec(memory_space=pltpu.SMEM),
            pl.BlockSpec((sb, steps, D_ATTN), seq3),
            pl.BlockSpec((sb, steps, D_KV), seq3),
            pl.BlockSpec((sb, steps, D_KV), seq3),
            pl.BlockSpec((None, sb, hist, D_KV), lambda i: (l, i, 0, 0)),
            pl.BlockSpec((None, sb, hist, D_KV), lambda i: (l, i, 0, 0)),
        ],
        out_specs=pl.BlockSpec((sb, steps, D_ATTN), seq3),
        out_shape=jax.ShapeDtypeStruct((db, steps, D_ATTN), BF16),
        compiler_params=_params("parallel"),
        name="attn_sample",
    )(sinks, q_s, k_new, v_new, cache_k4, cache_v4)


def _out_proj_kernel(c_ref, a_ref, w_ref, x_ref, o_ref):
    o_ref[...] = (x_ref[...]
                  + jnp.dot(c_ref[...], w_ref[0:D_CONV, :], preferred_element_type=F32)
                  + jnp.dot(a_ref[...], w_ref[D_CONV:, :], preferred_element_type=F32))


def _out_proj(c, a, w_out_bf, x_all, l, row0, tile):
    rows = c.shape[0]
    off = row0 // tile
    return pl.pallas_call(
        _out_proj_kernel,
        grid=(rows // tile,),
        in_specs=[
            pl.BlockSpec((tile, D_CONV), lambda i: (i, 0)),
            pl.BlockSpec((tile, D_ATTN), lambda i: (i, 0)),
            pl.BlockSpec((None, D_MODEL, D_MODEL), lambda i: (l, 0, 0)),
            pl.BlockSpec((tile, D_MODEL), lambda i: (off + i, 0)),
        ],
        out_specs=pl.BlockSpec((tile, D_MODEL), lambda i: (off + i, 0)),
        out_shape=jax.ShapeDtypeStruct(x_all.shape, F32),
        input_output_aliases={3: 0},
        compiler_params=_params("parallel"),
        name="out_proj",
    )(c, a, w_out_bf, x_all)


FF_CHUNK = 256


def _swiglu(x, g_ref, wgu_ref, wd_ref, act_ref):
    h = _rms(x, g_ref[...]).astype(BF16)
    for c0 in range(0, D_FF, FF_CHUNK):
        gate = jnp.dot(h, wgu_ref[:, c0:c0 + FF_CHUNK], preferred_element_type=F32)
        up = jnp.dot(h, wgu_ref[:, D_FF + c0:D_FF + c0 + FF_CHUNK], preferred_element_type=F32)
        act_ref[:, c0:c0 + FF_CHUNK] = (gate * _sigmoid(gate) * up).astype(BF16)
    return jnp.dot(act_ref[...], wd_ref[...], preferred_element_type=F32)


def _ffn_dense_kernel(x_ref, g_ref, wgu_ref, wd_ref, o_ref, act_ref):
    x = x_ref[...]
    o_ref[...] = x + _swiglu(x, g_ref, wgu_ref, wd_ref, act_ref)


def _ffn_dense(x_all, g_ffn3, wgu_bf, wd_bf, l, tile):
    n = x_all.shape[0]
    once = pl.Buffered(1)
    return pl.pallas_call(
        _ffn_dense_kernel,
        grid=(n // tile,),
        in_specs=[
            pl.BlockSpec((tile, D_MODEL), lambda i: (i, 0)),
            pl.BlockSpec((None, 1, D_MODEL), lambda i: (l, 0, 0)),
            pl.BlockSpec((None, D_MODEL, 2 * D_FF), lambda i: (l // 2, 0, 0), pipeline_mode=once),
            pl.BlockSpec((None, D_FF, D_MODEL), lambda i: (l // 2, 0, 0), pipeline_mode=once),
        ],
        out_specs=pl.BlockSpec((tile, D_MODEL), lambda i: (i, 0)),
        out_shape=jax.ShapeDtypeStruct(x_all.shape, F32),
        scratch_shapes=[pltpu.VMEM((tile, D_FF), BF16)],
        input_output_aliases={0: 0},
        compiler_params=_params("parallel"),
        name="ffn_dense",
    )(x_all, g_ffn3, wgu_bf, wd_bf)


INFO_E, INFO_G, INFO_R = 0, 2, 4


def _route_kernel(x_ref, g_ref, whl_ref, info_ref, cnt_ref, run_ref):
    @pl.when(pl.program_id(0) == 0)
    def _():
        run_ref[...] = jnp.zeros_like(run_ref)

    h = _rms(x_ref[...], g_ref[...])
    h_hi = h.astype(BF16)
    h_lo = (h - h_hi.astype(F32)).astype(BF16)
    dot = lambda a, b: jnp.dot(a, b, preferred_element_type=F32)
    hh = dot(h_hi, whl_ref[...])
    logits = hh[:, :LANES] + hh[:, LANES:] + dot(h_lo, whl_ref[:, :LANES])
    t = logits.shape[0]
    lane = lax.broadcasted_iota(I32, logits.shape, 1)
    lg = jnp.where(lane < N_EXPERTS, logits, -jnp.inf)
    m1 = jnp.max(lg, -1, keepdims=True)
    i1 = jnp.min(jnp.where(lg == m1, lane, LANES), -1, keepdims=True)
    lg2 = jnp.where(lane == i1, -jnp.inf, lg)
    m2 = jnp.max(lg2, -1, keepdims=True)
    i2 = jnp.min(jnp.where(lg2 == m2, lane, LANES), -1, keepdims=True)
    e = jnp.exp(m2 - m1)
    g1 = 1.0 / (1.0 + e)
    g2 = e / (1.0 + e)
    sel1 = lane == i1
    sel2 = lane == i2
    onehot = jnp.where(sel1 | sel2, 1.0, 0.0)
    r = lax.broadcasted_iota(I32, (t, t), 0)
    c = lax.broadcasted_iota(I32, (t, t), 1)
    tri = jnp.where(r > c, 1.0, 0.0).astype(BF16)
    before = dot(tri, onehot.astype(BF16)) + run_ref[0:1, :]
    r1 = jnp.sum(jnp.where(sel1, before, 0.0), -1, keepdims=True)
    r2 = jnp.sum(jnp.where(sel2, before, 0.0), -1, keepdims=True)
    run_ref[...] = run_ref[...] + jnp.sum(onehot, 0, keepdims=True)
    cnt_ref[...] = run_ref[...]
    info = jnp.zeros(logits.shape, F32)
    for pos, val in ((INFO_E, i1.astype(F32)), (INFO_E + 1, i2.astype(F32)), (INFO_G, g1), (INFO_G + 1, g2),
                     (INFO_R, r1), (INFO_R + 1, r2)):
        info = jnp.where(lane == pos, val, info)
    info_ref[...] = info


def _route(x_all, g_ffn3, wr_hl, l, tile):
    n = x_all.shape[0]
    return pl.pallas_call(
        _route_kernel,
        grid=(n // tile,),
        in_specs=[
            pl.BlockSpec((tile, D_MODEL), lambda i: (i, 0)),
            pl.BlockSpec((None, 1, D_MODEL), lambda i: (l, 0, 0)),
            pl.BlockSpec((None, D_MODEL, 2 * LANES), lambda i: (l // 2, 0, 0)),
        ],
        out_specs=[
            pl.BlockSpec((tile, LANES), lambda i: (i, 0)),
            pl.BlockSpec((SUBLANES, LANES), lambda i: (0, 0)),
        ],
        out_shape=[
            jax.ShapeDtypeStruct((n, LANES), F32),
            jax.ShapeDtypeStruct((SUBLANES, LANES), F32),
        ],
        scratch_shapes=[pltpu.VMEM((SUBLANES, LANES), F32)],
        compiler_params=_params("arbitrary"),
        name="route",
    )(x_all, g_ffn3, wr_hl)


SLAB = D_MODEL // LANES
assert SLAB == SUBLANES


def _slab(ref, row, rows=1):
    start = row * SLAB
    if not isinstance(start, int):
        start = pl.multiple_of(start, SLAB)
    return ref.at[pl.ds(start, rows * SLAB)]


def _to_slabs(slab_ref, x):
    for s in range(SLAB):
        slab_ref[pl.ds(s, x.shape[0], stride=SLAB), :] = x[:, s * LANES:(s + 1) * LANES]


def _from_slabs(slab_ref, rows):
    return jnp.concatenate([slab_ref[pl.ds(s, rows, stride=SLAB), :] for s in range(SLAB)], 1)


def _start_row_copies(tile, make_copy):
    def body(group, carry):
        for row in range(SUBLANES):
            for j in range(2):
                make_copy(group * SUBLANES + row, j).start(priority=row % 2)
        return carry

    lax.fori_loop(0, tile // SUBLANES, body, 0)


CAST_PARTS = 8


def _dispatch_kernel(pad_start_ref, pad_len_ref, pos_ref, x_ref, wgu_ref, wd_ref, xs_ref, wgu_bf_ref, wd_bf_ref,
                     rows_ref, zero_ref, sem, zsem, *, tile, tm):
    first = pl.program_id(0) == 0
    wgu_bf_ref[...] = wgu_ref[...].astype(BF16)
    wd_bf_ref[...] = wd_ref[...].astype(BF16)

    def pad_copies(action):
        for e in range(N_EXPERTS):
            start = pad_start_ref[e]
            run = tm // 2
            while run >= 1:
                take = (pad_len_ref[e] & run) != 0

                @pl.when(take)
                def _():
                    action(pltpu.make_async_copy(_slab(zero_ref, 0, run), _slab(xs_ref, start, run), zsem))

                start = start + jnp.where(take, run, 0)
                run //= 2
        n_tiles = xs_ref.shape[0] // (tm * SLAB)
        for t in range(n_tiles - (N_EXPERTS - 1), n_tiles):
            @pl.when(t >= pad_len_ref[N_EXPERTS])
            def _():
                for half in range(2):
                    action(pltpu.make_async_copy(zero_ref, _slab(xs_ref, t * tm + half * (tm // 2), tm // 2), zsem))

    @pl.when(first)
    def _():
        zero_ref[...] = jnp.zeros_like(zero_ref)
        pad_copies(lambda c: c.start())

    _to_slabs(rows_ref, x_ref[...])
    _start_row_copies(tile, lambda i, j: pltpu.make_async_copy(
        _slab(rows_ref, i), _slab(xs_ref, pos_ref[0, j, i]), sem.at[j]))
    for j in range(2):
        pltpu.make_async_copy(_slab(xs_ref, 0, tile), _slab(xs_ref, 0, tile), sem.at[j]).wait()

    @pl.when(first)
    def _():
        pad_copies(lambda c: c.wait())


def _dispatch(pad_start, pad_len, pos3, x_all, w_gu, w_down, l, n_rows, tile, tm):
    n = x_all.shape[0]
    steps = n // tile
    assert steps >= N_EXPERTS * CAST_PARTS
    m = l // 2
    gu_rows, d_rows = D_MODEL // CAST_PARTS, D_FF // CAST_PARTS

    def part(i):
        c = jnp.minimum(i, N_EXPERTS * CAST_PARTS - 1)
        return c // CAST_PARTS, c % CAST_PARTS

    grid_spec = pltpu.PrefetchScalarGridSpec(
        num_scalar_prefetch=2,
        grid=(steps,),
        in_specs=[
            pl.BlockSpec((1, 2, tile), lambda i, ps, pn: (i, 0, 0), memory_space=pltpu.SMEM),
            pl.BlockSpec((tile, D_MODEL), lambda i, ps, pn: (i, 0)),
            pl.BlockSpec((None, None, gu_rows, 2 * D_FF), lambda i, ps, pn: (m, *part(i), 0)),
            pl.BlockSpec((None, None, d_rows, D_MODEL), lambda i, ps, pn: (m, *part(i), 0)),
        ],
        out_specs=[
            pl.BlockSpec(memory_space=pl.ANY),
            pl.BlockSpec((None, gu_rows, 2 * D_FF), lambda i, ps, pn: (*part(i), 0)),
            pl.BlockSpec((None, d_rows, D_MODEL), lambda i, ps, pn: (*part(i), 0)),
        ],
        scratch_shapes=[pltpu.VMEM((tile * SLAB, LANES), F32), pltpu.VMEM((tm // 2 * SLAB, LANES), F32),
                        pltpu.SemaphoreType.DMA((2,)), pltpu.SemaphoreType.DMA(())],
    )
    return pl.pallas_call(
        functools.partial(_dispatch_kernel, tile=tile, tm=tm),
        grid_spec=grid_spec,
        out_shape=[
            jax.ShapeDtypeStruct((n_rows * SLAB, LANES), F32),
            jax.ShapeDtypeStruct((N_EXPERTS, D_MODEL, 2 * D_FF), BF16),
            jax.ShapeDtypeStruct((N_EXPERTS, D_FF, D_MODEL), BF16),
        ],
        compiler_params=_params("arbitrary"),
        name="moe_dispatch",
    )(pad_start, pad_len, pos3, x_all, w_gu, w_down)


def _moe_kernel(te_ref, used_ref, xs_ref, g_ref, wgu_ref, wd_ref, y_ref, act_ref, *, tm):
    used = pl.program_id(0) < used_ref[0]

    @pl.when(used)
    def _():
        _to_slabs(y_ref, _swiglu(_from_slabs(xs_ref, tm), g_ref, wgu_ref, wd_ref, act_ref))

    @pl.when(jnp.logical_not(used))
    def _():
        y_ref[...] = jnp.zeros_like(y_ref)


def _moe(tile_expert, n_used, xs, g_ffn3, wgu_bf, wd_bf, l, tm):
    nt = xs.shape[0] // (tm * SLAB)

    def tile_idx(t, used):
        return jnp.minimum(t, used[0] - 1)

    grid_spec = pltpu.PrefetchScalarGridSpec(
        num_scalar_prefetch=2,
        grid=(nt,),
        in_specs=[
            pl.BlockSpec((tm * SLAB, LANES), lambda t, te, used: (tile_idx(t, used), 0)),
            pl.BlockSpec((None, 1, D_MODEL), lambda t, te, used: (l, 0, 0)),
            pl.BlockSpec((None, D_MODEL, 2 * D_FF), lambda t, te, used: (te[tile_idx(t, used)], 0, 0)),
            pl.BlockSpec((None, D_FF, D_MODEL), lambda t, te, used: (te[tile_idx(t, used)], 0, 0)),
        ],
        out_specs=pl.BlockSpec((tm * SLAB, LANES), lambda t, te, used: (t, 0)),
        scratch_shapes=[pltpu.VMEM((tm, D_FF), BF16)],
    )
    return pl.pallas_call(
        functools.partial(_moe_kernel, tm=tm),
        grid_spec=grid_spec,
        out_shape=jax.ShapeDtypeStruct(xs.shape, F32),
        compiler_params=_params("arbitrary"),
        name="moe_experts",
    )(tile_expert, n_used, xs, g_ffn3, wgu_bf, wd_bf)


def _gather_combine(pos_ref, info_ref, x_ref, y_ref, buf_ref, sem, tile):
    _start_row_copies(tile, lambda i, j: pltpu.make_async_copy(
        _slab(y_ref, pos_ref[0, j, i]), _slab(buf_ref.at[j], i), sem.at[j]))
    for j in range(2):
        pltpu.make_async_copy(_slab(y_ref, 0, tile), _slab(y_ref, 0, tile), sem.at[j]).wait()
    info = info_ref[...]
    return (x_ref[...]
            + info[:, INFO_G:INFO_G + 1] * _from_slabs(buf_ref.at[0], tile)
            + info[:, INFO_G + 1:INFO_G + 2] * _from_slabs(buf_ref.at[1], tile))


def _combine_kernel(pos_ref, info_ref, x_ref, y_ref, o_ref, buf_ref, sem, *, tile):
    o_ref[...] = _gather_combine(pos_ref, info_ref, x_ref, y_ref, buf_ref, sem, tile)


def _combine_final_kernel(pos_ref, info_ref, x_ref, y_ref, g_ref, op_ref, os_ref, buf_ref, sem, *, tile, steps_p):
    out = _rms(_gather_combine(pos_ref, info_ref, x_ref, y_ref, buf_ref, sem, tile), g_ref[...])
    is_prompt = pl.program_id(0) < steps_p

    @pl.when(is_prompt)
    def _():
        op_ref[...] = out

    @pl.when(jnp.logical_not(is_prompt))
    def _():
        os_ref[...] = out


def _combine(pos3, info, x_all, y, tile, final=None):
    n = x_all.shape[0]
    row = lambda i: (i, 0)
    in_specs = [
        pl.BlockSpec((1, 2, tile), lambda i: (i, 0, 0), memory_space=pltpu.SMEM),
        pl.BlockSpec((tile, LANES), row),
        pl.BlockSpec((tile, D_MODEL), row),
        pl.BlockSpec(memory_space=pl.ANY),
    ]
    scratch = [pltpu.VMEM((2, tile * SLAB, LANES), F32), pltpu.SemaphoreType.DMA((2,))]
    if final is None:
        return pl.pallas_call(
            functools.partial(_combine_kernel, tile=tile),
            grid=(n // tile,),
            in_specs=in_specs,
            out_specs=pl.BlockSpec((tile, D_MODEL), row),
            out_shape=jax.ShapeDtypeStruct(x_all.shape, F32),
            scratch_shapes=scratch,
            input_output_aliases={2: 0},
            compiler_params=_params("arbitrary"),
            name="moe_combine",
        )(pos3, info, x_all, y)
    g_fin, n_p = final
    steps_p = n_p // tile
    return pl.pallas_call(
        functools.partial(_combine_final_kernel, tile=tile, steps_p=steps_p),
        grid=(n // tile,),
        in_specs=in_specs + [pl.BlockSpec((1, D_MODEL), lambda i: (0, 0))],
        out_specs=[
            pl.BlockSpec((tile, D_MODEL), lambda i: (jnp.minimum(i, steps_p - 1), 0)),
            pl.BlockSpec((tile, D_MODEL), lambda i: (jnp.maximum(i - steps_p, 0), 0)),
        ],
        out_shape=[jax.ShapeDtypeStruct((n_p, D_MODEL), F32), jax.ShapeDtypeStruct((n - n_p, D_MODEL), F32)],
        scratch_shapes=scratch,
        compiler_params=_params("arbitrary"),
        name="moe_combine_final",
    )(pos3, info, x_all, y, g_fin)


def _ffn_moe(x_all, g_ffn3, wr_hl, w_moe_gu, w_moe_down, l, tile, tm, final=None):
    n = x_all.shape[0]
    td = next(t for t in (256, 128, 64, 32, 16, 8) if n % t == 0 and n // t >= N_EXPERTS * CAST_PARTS)
    info, cnt = _route(x_all, g_ffn3, wr_hl, l, tile)
    counts = cnt[0, :N_EXPERTS].astype(I32)
    tiles_per = (counts + tm - 1) // tm
    tile_end = jnp.cumsum(tiles_per)
    base = (tile_end - tiles_per) * tm
    experts = info[:, INFO_E:INFO_E + 2].astype(I32)
    ranks = info[:, INFO_R:INFO_R + 2].astype(I32)
    pos = jnp.sum(jnp.where(experts[..., None] == jnp.arange(N_EXPERTS), base, 0), -1) + ranks
    n_tiles = (2 * n + N_EXPERTS * (tm - 1)) // tm
    n_used = tile_end[-1:]
    tile_expert = jnp.minimum(jnp.sum(jnp.arange(n_tiles)[:, None] >= tile_end[None, :], -1), N_EXPERTS - 1)
    blocked = lambda t: pos.T.reshape(2, n // t, t).transpose(1, 0, 2)
    pad_len = jnp.concatenate([tiles_per * tm - counts, n_used])
    xs, wgu_bf, wd_bf = _dispatch(base + counts, pad_len, blocked(td), x_all, w_moe_gu, w_moe_down, l,
                                  n_tiles * tm, td, tm)
    y = _moe(tile_expert.astype(I32), n_used.astype(I32), xs, g_ffn3, wgu_bf, wd_bf, l, tm)
    return _combine(blocked(tile), info, x_all, y, tile, final)


def _final_kernel(x_ref, g_ref, o_ref):
    o_ref[...] = _rms(x_ref[...], g_ref[...])


def _final_norm(x_all, g2, row0, rows, tile):
    off = row0 // tile
    return pl.pallas_call(
        _final_kernel,
        grid=(rows // tile,),
        in_specs=[
            pl.BlockSpec((tile, D_MODEL), lambda i: (off + i, 0)),
            pl.BlockSpec((1, D_MODEL), lambda i: (0, 0)),
        ],
        out_specs=pl.BlockSpec((tile, D_MODEL), lambda i: (i, 0)),
        out_shape=jax.ShapeDtypeStruct((rows, D_MODEL), F32),
        compiler_params=_params("parallel"),
        name="final_norm",
    )(x_all, g2)


def _rope_tables(pos):
    half = HEAD_DIM // 2
    inv = ROPE_THETA ** (-jnp.arange(half, dtype=F32) / half)
    ang = pos.astype(F32)[:, None] * inv[None, :]
    cos = jnp.cos(ang)
    sin = jnp.sin(ang)
    reps = LANES // HEAD_DIM
    return jnp.tile(jnp.concatenate([cos, cos], -1), (1, reps)), jnp.tile(jnp.concatenate([-sin, sin], -1), (1, reps))


def kernel(x_prompt, x_sample, cache_conv, cache_swa_k, cache_swa_v, g_mix, w_in, conv_w, conv_b, conv_ln_g,
           conv_ln_b, attn_sinks, w_out, g_ffn, w_dense_gu, w_dense_down, w_router, w_moe_gu, w_moe_down, g_final):
    batch, seq, _ = x_prompt.shape
    db, steps, _ = x_sample.shape
    depth = g_mix.shape[0]
    hist = cache_swa_k.shape[2]
    n_p, n_s = batch * seq, db * steps
    n = n_p + n_s
    assert seq % BLOCK == 0 and seq >= CONV_WIDTH - 1 and steps <= hist and steps < CONV_WIDTH - 1
    assert hist == min(WINDOW, PAST_LEN)
    tile = _pick_tile((1024, 512, 256, 128), n_p, n_s)
    tq = _pick_tile((512, 128), seq)
    tm = _pick_tile((512, 256), 2 * n)
    sb_attn = _pick_tile((16, 8), db)
    assert db % SUBLANES == 0
    conv_wb = jnp.broadcast_to(conv_w[:, :, None, :], (depth, CONV_WIDTH, SUBLANES, D_CONV))

    x_all = jnp.concatenate([x_prompt.reshape(n_p, D_MODEL), x_sample.reshape(n_s, D_MODEL)], 0)
    pos_all = jnp.concatenate([jnp.tile(jnp.arange(seq, dtype=I32), batch),
                               jnp.tile(PAST_LEN + jnp.arange(steps, dtype=I32), db)])
    cos_t, sin_t = _rope_tables(pos_all)

    vec3 = lambda a: a.reshape(a.shape[0], 1, a.shape[1])
    g_mix3, g_ffn3, conv_b3, ln_g3, ln_b3 = map(vec3, (g_mix, g_ffn, conv_b, conv_ln_g, conv_ln_b))
    w_in_bf, w_out_bf = w_in.astype(BF16), w_out.astype(BF16)
    wgu_bf, wd_bf = w_dense_gu.astype(BF16), w_dense_down.astype(BF16)
    wr = jnp.pad(w_router, ((0, 0), (0, 0), (0, LANES - N_EXPERTS)))
    wr_hi = wr.astype(BF16)
    wr_hl = jnp.concatenate([wr_hi, (wr - wr_hi.astype(F32)).astype(BF16)], -1)
    cache_k4 = cache_swa_k.reshape(depth, db, hist, D_KV)
    cache_v4 = cache_swa_v.reshape(depth, db, hist, D_KV)

    def tail_rows(a, rows):
        return jnp.stack([a[(b + 1) * seq - rows:(b + 1) * seq] for b in range(batch)], 0)

    keep = min(WINDOW, seq)
    g_fin = g_final.reshape(1, D_MODEL)
    conv_p, k_p, v_p, u_new, k_new, v_new = [], [], [], [], [], []
    for l in range(depth):
        u, q, k, v = _in_proj(x_all, g_mix3, w_in_bf, cos_t, sin_t, l, tile)

        c_p = _conv_prompt(u, conv_wb, conv_b3, ln_g3, ln_b3, l, batch, seq, tq)
        x_all = _attn_out_prompt(attn_sinks, q, k, v, c_p, w_out_bf, x_all, l, batch, seq, tq)

        u_s = u[n_p:].reshape(db, steps, D_CONV)
        k_s = k[n_p:].reshape(db, steps, D_KV)
        v_s = v[n_p:].reshape(db, steps, D_KV)
        c_s = _conv_sample(cache_conv, u_s, conv_wb, conv_b3, ln_g3, ln_b3, l).reshape(n_s, D_CONV)
        a_s = _attn_sample(attn_sinks, q[n_p:].reshape(db, steps, D_ATTN), k_s, v_s, cache_k4, cache_v4,
                           l, sb_attn).reshape(n_s, D_ATTN)
        x_all = _out_proj(c_s, a_s, w_out_bf, x_all, l, n_p, tile)

        conv_p.append(tail_rows(u, CONV_WIDTH - 1))
        k_p.append(tail_rows(k, keep).reshape(batch, keep, N_KV_HEADS, HEAD_DIM))
        v_p.append(tail_rows(v, keep).reshape(batch, keep, N_KV_HEADS, HEAD_DIM))
        u_new.append(u_s)
        k_new.append(k_s.reshape(db, steps, N_KV_HEADS, HEAD_DIM))
        v_new.append(v_s.reshape(db, steps, N_KV_HEADS, HEAD_DIM))

        if l % 2 == 0:
            x_all = _ffn_dense(x_all, g_ffn3, wgu_bf, wd_bf, l, tile)
        elif l < depth - 1:
            x_all = _ffn_moe(x_all, g_ffn3, wr_hl, w_moe_gu, w_moe_down, l, tile, tm)
        else:
            y_p, y_s = _ffn_moe(x_all, g_ffn3, wr_hl, w_moe_gu, w_moe_down, l, tile, tm, final=(g_fin, n_p))

    if depth % 2:
        y_p, y_s = _final_norm(x_all, g_fin, 0, n_p, tile), _final_norm(x_all, g_fin, n_p, n_s, tile)
    y_prompt = y_p.reshape(batch, seq, D_MODEL)
    y_sample = y_s.reshape(db, steps, D_MODEL)
    state_s = lambda cache, new: jnp.concatenate([cache[:, :, steps:], jnp.stack(new, 0)], 2)
    return (y_prompt, y_sample, jnp.stack(conv_p, 0), jnp.stack(k_p, 0), jnp.stack(v_p, 0),
            state_s(cache_conv, u_new), state_s(cache_swa_k, k_new), state_s(cache_swa_v, v_new))
```

```python
import functools

import jax
import jax.numpy as jnp
from jax import lax
from jax.experimental import pallas as pl
from jax.experimental.pallas import tpu as pltpu

F32 = jnp.float32
BF16 = jnp.bfloat16
I32 = jnp.int32

D_MODEL = 1024
D_CONV = 512
N_HEADS = 8
HEAD_DIM = 64
N_KV_HEADS = 2
GROUP = N_HEADS // N_KV_HEADS
D_ATTN = N_HEADS * HEAD_DIM
D_KV = N_KV_HEADS * HEAD_DIM
D_IN = 2 * D_CONV + D_ATTN + 2 * D_KV
CONV_WIDTH = 31
WINDOW = 128
BLOCK = 128
ROPE_THETA = 10000.0
D_FF = 2816
N_EXPERTS = 8
EPS = 1e-6
PAST_LEN = 8192

LANES = 128
SUBLANES = 8
CONV_HALO = 32
CONV_PAD = CONV_HALO - (CONV_WIDTH - 1)
VMEM_LIMIT = 56 * 1024 * 1024


def _pick_tile(cands, *sizes):
    for c in cands:
        if all(s % c == 0 for s in sizes):
            return c
    raise ValueError(f"no tile in {cands} divides {sizes}")


def _params(*sem):
    return pltpu.CompilerParams(dimension_semantics=sem, vmem_limit_bytes=VMEM_LIMIT)


def _rms(x, g):
    return x * lax.rsqrt(jnp.mean(x * x, -1, keepdims=True) + EPS) * g


def _sigmoid(x):
    return 1.0 / (1.0 + jnp.exp(-x))


def _in_proj_kernel(x_ref, g_ref, w_ref, cos_ref, sin_ref, u_ref, q_ref, k_ref, v_ref):
    h = _rms(x_ref[...], g_ref[...]).astype(BF16)
    p = jnp.dot(h, w_ref[...], preferred_element_type=F32)
    u_ref[...] = p[:, :D_CONV] * _sigmoid(p[:, D_CONV:2 * D_CONV])
    cos = cos_ref[...]
    sin = sin_ref[...]
    lane = lax.broadcasted_iota(I32, cos.shape, 1)
    first_half = (lane % HEAD_DIM) < (HEAD_DIM // 2)

    def rope(xc):
        partner = jnp.where(first_half,
                            pltpu.roll(xc, LANES - HEAD_DIM // 2, 1),
                            pltpu.roll(xc, HEAD_DIM // 2, 1))
        return xc * cos + partner * sin

    q0 = 2 * D_CONV
    for c in range(D_ATTN // LANES):
        qc = rope(p[:, q0 + c * LANES:q0 + (c + 1) * LANES])
        q_ref[:, c * LANES:(c + 1) * LANES] = (qc * (HEAD_DIM ** -0.5)).astype(BF16)
    k0 = q0 + D_ATTN
    k_ref[...] = rope(p[:, k0:k0 + D_KV])
    v_ref[...] = p[:, k0 + D_KV:k0 + 2 * D_KV]


def _in_proj(x_all, g_mix3, w_in_bf, cos_t, sin_t, l, tile):
    n = x_all.shape[0]
    row = lambda i: (i, 0)
    return pl.pallas_call(
        _in_proj_kernel,
        grid=(n // tile,),
        in_specs=[
            pl.BlockSpec((tile, D_MODEL), row),
            pl.BlockSpec((None, 1, D_MODEL), lambda i: (l, 0, 0)),
            pl.BlockSpec((None, D_MODEL, D_IN), lambda i: (l, 0, 0)),
            pl.BlockSpec((tile, LANES), row),
            pl.BlockSpec((tile, LANES), row),
        ],
        out_specs=[
            pl.BlockSpec((tile, D_CONV), row),
            pl.BlockSpec((tile, D_ATTN), row),
            pl.BlockSpec((tile, D_KV), row),
            pl.BlockSpec((tile, D_KV), row),
        ],
        out_shape=[
            jax.ShapeDtypeStruct((n, D_CONV), F32),
            jax.ShapeDtypeStruct((n, D_ATTN), BF16),
            jax.ShapeDtypeStruct((n, D_KV), F32),
            jax.ShapeDtypeStruct((n, D_KV), F32),
        ],
        compiler_params=_params("parallel"),
        name="in_proj",
    )(x_all, g_mix3, w_in_bf, cos_t, sin_t)


def _ln_swish(y, g, b):
    mu = jnp.mean(y, -1, keepdims=True)
    yc = y - mu
    z = yc * lax.rsqrt(jnp.mean(yc * yc, -1, keepdims=True) + EPS) * g + b
    return z * _sigmoid(z)


def _conv_prompt_kernel(u_ref, wb_ref, cb_ref, lg_ref, lb_ref, c_ref, ext_ref, sh_ref, y_ref, *, tile, rb):
    @pl.when(pl.program_id(1) == 0)
    def _():
        ext_ref[0:CONV_HALO, :] = jnp.zeros((CONV_HALO, D_CONV), F32)

    ext_ref[CONV_HALO:CONV_HALO + tile, :] = u_ref[...]
    for b in range(1, SUBLANES):
        sh_ref[b - 1] = ext_ref[pl.ds(b, sh_ref.shape[1]), :]
    def row_block(i, carry):
        r0 = pl.multiple_of(i * rb, rb)
        accs = [jnp.broadcast_to(cb_ref[...], (SUBLANES, D_CONV))] * (rb // SUBLANES)
        for k in range(CONV_WIDTH):
            b = (CONV_PAD + k) % SUBLANES
            wk = wb_ref[k]
            for rg in range(rb // SUBLANES):
                row = pl.multiple_of(r0 + (CONV_PAD + k - b) + rg * SUBLANES, SUBLANES)
                src = ext_ref[pl.ds(row, SUBLANES), :] if b == 0 else sh_ref[b - 1, pl.ds(row, SUBLANES), :]
                accs[rg] = accs[rg] + src * wk
        for rg in range(rb // SUBLANES):
            y_ref[pl.ds(pl.multiple_of(r0 + rg * SUBLANES, SUBLANES), SUBLANES), :] = accs[rg]
        return carry

    lax.fori_loop(0, tile // rb, row_block, 0)
    c_ref[...] = _ln_swish(y_ref[...], lg_ref[...], lb_ref[...]).astype(BF16)
    ext_ref[0:CONV_HALO, :] = ext_ref[tile:tile + CONV_HALO, :]


def _conv_prompt(u_all, conv_wb, conv_b3, ln_g3, ln_b3, l, batch, seq, tile):
    nt = seq // tile
    vec = pl.BlockSpec((None, 1, D_CONV), lambda b, j: (l, 0, 0))
    return pl.pallas_call(
        functools.partial(_conv_prompt_kernel, tile=tile, rb=32),
        grid=(batch, nt),
        in_specs=[
            pl.BlockSpec((tile, D_CONV), lambda b, j: (b * nt + j, 0)),
            pl.BlockSpec((None, CONV_WIDTH, SUBLANES, D_CONV), lambda b, j: (l, 0, 0, 0)),
            vec, vec, vec,
        ],
        out_specs=pl.BlockSpec((tile, D_CONV), lambda b, j: (b * nt + j, 0)),
        out_shape=jax.ShapeDtypeStruct((batch * seq, D_CONV), BF16),
        scratch_shapes=[pltpu.VMEM((tile + CONV_HALO, D_CONV), F32),
                        pltpu.VMEM((SUBLANES - 1, tile + CONV_HALO - SUBLANES, D_CONV), F32),
                        pltpu.VMEM((tile, D_CONV), F32)],
        compiler_params=_params("arbitrary", "arbitrary"),
        name="conv_prompt",
    )(u_all, conv_wb, conv_b3, ln_g3, ln_b3)


def _conv_sample_kernel(hist_ref, u_ref, wb_ref, cb_ref, lg_ref, lb_ref, c_ref):
    sb, steps, _ = u_ref.shape
    hist = CONV_WIDTH - 1
    accs = [jnp.broadcast_to(cb_ref[...], (sb, D_CONV)) for _ in range(steps)]
    for r in range(hist + steps):
        row = hist_ref[:, r, :] if r < hist else u_ref[:, r - hist, :]
        for t in range(steps):
            if 0 <= r - t < CONV_WIDTH:
                accs[t] = accs[t] + row * wb_ref[r - t]
    for t in range(steps):
        c_ref[:, t, :] = _ln_swish(accs[t], lg_ref[...], lb_ref[...]).astype(BF16)


def _conv_sample(cache_conv, u_s, conv_wb, conv_b3, ln_g3, ln_b3, l):
    db, steps, _ = u_s.shape
    sb = SUBLANES
    vec = pl.BlockSpec((None, 1, D_CONV), lambda i: (l, 0, 0))
    return pl.pallas_call(
        _conv_sample_kernel,
        grid=(db // sb,),
        in_specs=[
            pl.BlockSpec((None, sb, CONV_WIDTH - 1, D_CONV), lambda i: (l, i, 0, 0)),
            pl.BlockSpec((sb, steps, D_CONV), lambda i: (i, 0, 0)),
            pl.BlockSpec((None, CONV_WIDTH, SUBLANES, D_CONV), lambda i: (l, 0, 0, 0)),
            vec, vec, vec,
        ],
        out_specs=pl.BlockSpec((sb, steps, D_CONV), lambda i: (i, 0, 0)),
        out_shape=jax.ShapeDtypeStruct((db, steps, D_CONV), BF16),
        compiler_params=_params("parallel"),
        name="conv_sample",
    )(cache_conv, u_s, conv_wb, conv_b3, ln_g3, ln_b3)


def _softmax_sink_pv(s, mask, sink, v_bf, dot_pv):
    s = jnp.where(mask, s, -jnp.inf)
    m = jnp.maximum(jnp.max(s, -1, keepdims=True), sink)
    p = jnp.exp(s - m)
    den = jnp.sum(p, -1, keepdims=True) + jnp.exp(sink - m)
    return dot_pv(p.astype(BF16), v_bf) / den


def _attn_prompt_kernel(sink_ref, q_ref, kp_ref, kc_ref, vp_ref, vc_ref, c_ref, w_ref, x_ref, xo_ref, o_ref,
                        *, l, tq):
    first = pl.program_id(1) == 0
    cols = GROUP * BLOCK
    kj = lax.broadcasted_iota(I32, (2 * BLOCK, cols), 0)
    qi = lax.broadcasted_iota(I32, (2 * BLOCK, cols), 1) % BLOCK
    band = (kj > qi) & (kj <= qi + BLOCK)
    head_of_col = lax.broadcasted_iota(I32, (1, cols), 1) // BLOCK
    for qb in range(tq // BLOCK):
        if qb == 0:
            kk = jnp.concatenate([kp_ref[...], kc_ref[0:BLOCK, :]], 0)
            vv = jnp.concatenate([vp_ref[...], vc_ref[0:BLOCK, :]], 0)
            mask = band & (kj >= jnp.where(first, BLOCK, 0))
        else:
            kk = kc_ref[(qb - 1) * BLOCK:(qb + 1) * BLOCK, :]
            vv = vc_ref[(qb - 1) * BLOCK:(qb + 1) * BLOCK, :]
            mask = band
        kk = kk.astype(BF16)
        q = q_ref[qb * BLOCK:(qb + 1) * BLOCK, :]
        for g in range(N_KV_HEADS):
            hs = [g * GROUP + i for i in range(GROUP)]
            q4 = jnp.concatenate([q[:, h * HEAD_DIM:(h + 1) * HEAD_DIM] for h in hs], 0)
            sink = jnp.zeros((1, cols), F32)
            for i, h in enumerate(hs):
                sink = jnp.where(head_of_col == i, sink_ref[l, h], sink)
            s = lax.dot_general(kk[:, g * HEAD_DIM:(g + 1) * HEAD_DIM], q4, (((1,), (1,)), ((), ())),
                                preferred_element_type=F32)
            s = jnp.where(mask, s, -jnp.inf)
            m = jnp.maximum(jnp.max(s, 0, keepdims=True), sink)
            p = jnp.exp(s - m)
            den = jnp.sum(p, 0, keepdims=True) + jnp.exp(sink - m)
            v_t = vv[:, g * HEAD_DIM:(g + 1) * HEAD_DIM].T.astype(BF16)
            o_t = jnp.dot(v_t, p.astype(BF16), preferred_element_type=F32) * (1.0 / den)
            for i, h in enumerate(hs):
                o_ref[qb * BLOCK:(qb + 1) * BLOCK, h * HEAD_DIM:(h + 1) * HEAD_DIM] = (
                    o_t[:, i * BLOCK:(i + 1) * BLOCK].T.astype(BF16))
        rows = slice(qb * BLOCK, (qb + 1) * BLOCK)
        xo_ref[rows, :] = (x_ref[rows, :]
                           + jnp.dot(c_ref[rows, :], w_ref[0:D_CONV, :], preferred_element_type=F32)
                           + jnp.dot(o_ref[rows, :], w_ref[D_CONV:, :], preferred_element_type=F32))


def _attn_out_prompt(sinks, q_all, k_all, v_all, c_p, w_out_bf, x_all, l, batch, seq, tq):
    nq = seq // tq
    per = tq // BLOCK
    cur = lambda b, j: (b * nq + j, 0)
    prev = lambda b, j: (jnp.maximum((b * nq + j) * per - 1, 0), 0)
    return pl.pallas_call(
        functools.partial(_attn_prompt_kernel, l=l, tq=tq),
        grid=(batch, nq),
        in_specs=[
            pl.BlockSpec(memory_space=pltpu.SMEM),
            pl.BlockSpec((tq, D_ATTN), cur),
            pl.BlockSpec((BLOCK, D_KV), prev),
            pl.BlockSpec((tq, D_KV), cur),
            pl.BlockSpec((BLOCK, D_KV), prev),
            pl.BlockSpec((tq, D_KV), cur),
            pl.BlockSpec((tq, D_CONV), cur),
            pl.BlockSpec((None, D_MODEL, D_MODEL), lambda b, j: (l, 0, 0)),
            pl.BlockSpec((tq, D_MODEL), cur),
        ],
        out_specs=pl.BlockSpec((tq, D_MODEL), cur),
        out_shape=jax.ShapeDtypeStruct(x_all.shape, F32),
        scratch_shapes=[pltpu.VMEM((tq, D_ATTN), BF16)],
        input_output_aliases={8: 0},
        compiler_params=_params("parallel", "parallel"),
        name="attn_out_prompt",
    )(sinks, q_all, k_all, k_all, v_all, v_all, c_p, w_out_bf, x_all)


def _attn_sample_kernel(sink_ref, q_ref, kn_ref, vn_ref, kc_ref, vc_ref, o_ref, *, l, steps):
    sb = q_ref.shape[0]
    hist = kc_ref.shape[1]
    rows = GROUP * steps
    pad = jnp.zeros((sb, hist - steps, D_KV), F32)
    kk = jnp.concatenate([kc_ref[...], kn_ref[...], pad], 1).astype(BF16)
    vv = jnp.concatenate([vc_ref[...], vn_ref[...], pad], 1).astype(BF16)
    t = lax.broadcasted_iota(I32, (sb, rows, 2 * hist), 1) % steps
    j = lax.broadcasted_iota(I32, (sb, rows, 2 * hist), 2)
    mask = ((j < hist) & (t + hist - j < WINDOW)) | ((j >= hist) & (j - hist <= t) & (t - (j - hist) < WINDOW))
    head_of_row = lax.broadcasted_iota(I32, (rows, 1), 0) // steps
    dot_qk = lambda a, b: jnp.einsum("bqd,bkd->bqk", a, b, preferred_element_type=F32)
    dot_pv = lambda a, b: jnp.einsum("bqk,bkd->bqd", a, b, preferred_element_type=F32)
    q = q_ref[...]
    for g in range(N_KV_HEADS):
        hs = [g * GROUP + i for i in range(GROUP)]
        q4 = jnp.concatenate([q[:, :, h * HEAD_DIM:(h + 1) * HEAD_DIM] for h in hs], 1)
        sink = jnp.zeros((rows, 1), F32)
        for i, h in enumerate(hs):
            sink = jnp.where(head_of_row == i, sink_ref[l, h], sink)
        s = dot_qk(q4, kk[:, :, g * HEAD_DIM:(g + 1) * HEAD_DIM])
        o = _softmax_sink_pv(s, mask, sink[None], vv[:, :, g * HEAD_DIM:(g + 1) * HEAD_DIM], dot_pv)
        for i, h in enumerate(hs):
            o_ref[:, :, h * HEAD_DIM:(h + 1) * HEAD_DIM] = o[:, i * steps:(i + 1) * steps, :].astype(BF16)


def _attn_sample(sinks, q_s, k_new, v_new, cache_k4, cache_v4, l, sb):
    db, steps, _ = q_s.shape
    hist = cache_k4.shape[2]
    seq3 = lambda i: (i, 0, 0)
    return pl.pallas_call(
        functools.partial(_attn_sample_kernel, l=l, steps=steps),
        grid=(db // sb,),
        in_specs=[
            pl.BlockSpec(memory_space=pltpu.SMEM),
            pl.BlockSpec((sb, steps, D_ATTN), seq3),
            pl.BlockSpec((sb, steps, D_KV), seq3),
            pl.BlockSpec((sb, steps, D_KV), seq3),
            pl.BlockSpec((None, sb, hist, D_KV), lambda i: (l, i, 0, 0)),
            pl.BlockSpec((None, sb, hist, D_KV), lambda i: (l, i, 0, 0)),
        ],
        out_specs=pl.BlockSpec((sb, steps, D_ATTN), seq3),
        out_shape=jax.ShapeDtypeStruct((db, steps, D_ATTN), BF16),
        compiler_params=_params("parallel"),
        name="attn_sample",
    )(sinks, q_s, k_new, v_new, cache_k4, cache_v4)


def _out_proj_kernel(c_ref, a_ref, w_ref, x_ref, o_ref):
    o_ref[...] = (x_ref[...]
                  + jnp.dot(c_ref[...], w_ref[0:D_CONV, :], preferred_element_type=F32)
                  + jnp.dot(a_ref[...], w_ref[D_CONV:, :], preferred_element_type=F32))


def _out_proj(c, a, w_out_bf, x_all, l, row0, tile):
    rows = c.shape[0]
    off = row0 // tile
    return pl.pallas_call(
        _out_proj_kernel,
        grid=(rows // tile,),
        in_specs=[
            pl.BlockSpec((tile, D_CONV), lambda i: (i, 0)),
            pl.BlockSpec((tile, D_ATTN), lambda i: (i, 0)),
            pl.BlockSpec((None, D_MODEL, D_MODEL), lambda i: (l, 0, 0)),
            pl.BlockSpec((tile, D_MODEL), lambda i: (off + i, 0)),
        ],
        out_specs=pl.BlockSpec((tile, D_MODEL), lambda i: (off + i, 0)),
        out_shape=jax.ShapeDtypeStruct(x_all.shape, F32),
        input_output_aliases={3: 0},
        compiler_params=_params("parallel"),
        name="out_proj",
    )(c, a, w_out_bf, x_all)


FF_CHUNK = 256


def _swiglu(x, g_ref, wgu_ref, wd_ref, act_ref):
    h = _rms(x, g_ref[...]).astype(BF16)
    for c0 in range(0, D_FF, FF_CHUNK):
        gate = jnp.dot(h, wgu_ref[:, c0:c0 + FF_CHUNK], preferred_element_type=F32)
        up = jnp.dot(h, wgu_ref[:, D_FF + c0:D_FF + c0 + FF_CHUNK], preferred_element_type=F32)
        act_ref[:, c0:c0 + FF_CHUNK] = (gate * _sigmoid(gate) * up).astype(BF16)
    return jnp.dot(act_ref[...], wd_ref[...], preferred_element_type=F32)


def _ffn_dense_kernel(x_ref, g_ref, wgu_ref, wd_ref, o_ref, act_ref):
    x = x_ref[...]
    o_ref[...] = x + _swiglu(x, g_ref, wgu_ref, wd_ref, act_ref)


def _ffn_dense(x_all, g_ffn3, wgu_bf, wd_bf, l, tile):
    n = x_all.shape[0]
    once = pl.Buffered(1)
    return pl.pallas_call(
        _ffn_dense_kernel,
        grid=(n // tile,),
        in_specs=[
            pl.BlockSpec((tile, D_MODEL), lambda i: (i, 0)),
            pl.BlockSpec((None, 1, D_MODEL), lambda i: (l, 0, 0)),
            pl.BlockSpec((None, D_MODEL, 2 * D_FF), lambda i: (l // 2, 0, 0), pipeline_mode=once),
            pl.BlockSpec((None, D_FF, D_MODEL), lambda i: (l // 2, 0, 0), pipeline_mode=once),
        ],
        out_specs=pl.BlockSpec((tile, D_MODEL), lambda i: (i, 0)),
        out_shape=jax.ShapeDtypeStruct(x_all.shape, F32),
        scratch_shapes=[pltpu.VMEM((tile, D_FF), BF16)],
        input_output_aliases={0: 0},
        compiler_params=_params("parallel"),
        name="ffn_dense",
    )(x_all, g_ffn3, wgu_bf, wd_bf)


INFO_E, INFO_G, INFO_R = 0, 2, 4


def _route_kernel(x_ref, g_ref, whl_ref, info_ref, cnt_ref, run_ref):
    @pl.when(pl.program_id(0) == 0)
    def _():
        run_ref[...] = jnp.zeros_like(run_ref)

    h = _rms(x_ref[...], g_ref[...])
    h_hi = h.astype(BF16)
    h_lo = (h - h_hi.astype(F32)).astype(BF16)
    dot = lambda a, b: jnp.dot(a, b, preferred_element_type=F32)
    hh = dot(h_hi, whl_ref[...])
    logits = hh[:, :LANES] + hh[:, LANES:] + dot(h_lo, whl_ref[:, :LANES])
    t = logits.shape[0]
    lane = lax.broadcasted_iota(I32, logits.shape, 1)
    lg = jnp.where(lane < N_EXPERTS, logits, -jnp.inf)
    m1 = jnp.max(lg, -1, keepdims=True)
    i1 = jnp.min(jnp.where(lg == m1, lane, LANES), -1, keepdims=True)
    lg2 = jnp.where(lane == i1, -jnp.inf, lg)
    m2 = jnp.max(lg2, -1, keepdims=True)
    i2 = jnp.min(jnp.where(lg2 == m2, lane, LANES), -1, keepdims=True)
    e = jnp.exp(m2 - m1)
    g1 = 1.0 / (1.0 + e)
    g2 = e / (1.0 + e)
    sel1 = lane == i1
    sel2 = lane == i2
    onehot = jnp.where(sel1 | sel2, 1.0, 0.0)
    r = lax.broadcasted_iota(I32, (t, t), 0)
    c = lax.broadcasted_iota(I32, (t, t), 1)
    tri = jnp.where(r > c, 1.0, 0.0).astype(BF16)
    before = dot(tri, onehot.astype(BF16)) + run_ref[0:1, :]
    r1 = jnp.sum(jnp.where(sel1, before, 0.0), -1, keepdims=True)
    r2 = jnp.sum(jnp.where(sel2, before, 0.0), -1, keepdims=True)
    run_ref[...] = run_ref[...] + jnp.sum(onehot, 0, keepdims=True)
    cnt_ref[...] = run_ref[...]
    info = jnp.zeros(logits.shape, F32)
    for pos, val in ((INFO_E, i1.astype(F32)), (INFO_E + 1, i2.astype(F32)), (INFO_G, g1), (INFO_G + 1, g2),
                     (INFO_R, r1), (INFO_R + 1, r2)):
        info = jnp.where(lane == pos, val, info)
    info_ref[...] = info


def _route(x_all, g_ffn3, wr_hl, l, tile):
    n = x_all.shape[0]
    return pl.pallas_call(
        _route_kernel,
        grid=(n // tile,),
        in_specs=[
            pl.BlockSpec((tile, D_MODEL), lambda i: (i, 0)),
            pl.BlockSpec((None, 1, D_MODEL), lambda i: (l, 0, 0)),
            pl.BlockSpec((None, D_MODEL, 2 * LANES), lambda i: (l // 2, 0, 0)),
        ],
        out_specs=[
            pl.BlockSpec((tile, LANES), lambda i: (i, 0)),
            pl.BlockSpec((SUBLANES, LANES), lambda i: (0, 0)),
        ],
        out_shape=[
            jax.ShapeDtypeStruct((n, LANES), F32),
            jax.ShapeDtypeStruct((SUBLANES, LANES), F32),
        ],
        scratch_shapes=[pltpu.VMEM((SUBLANES, LANES), F32)],
        compiler_params=_params("arbitrary"),
        name="route",
    )(x_all, g_ffn3, wr_hl)


SLAB = D_MODEL // LANES
assert SLAB == SUBLANES


def _slab(ref, row, rows=1):
    start = row * SLAB
    if not isinstance(start, int):
        start = pl.multiple_of(start, SLAB)
    return ref.at[pl.ds(start, rows * SLAB)]


def _to_slabs(slab_ref, x):
    for s in range(SLAB):
        slab_ref[pl.ds(s, x.shape[0], stride=SLAB), :] = x[:, s * LANES:(s + 1) * LANES]


def _from_slabs(slab_ref, rows):
    return jnp.concatenate([slab_ref[pl.ds(s, rows, stride=SLAB), :] for s in range(SLAB)], 1)


def _start_row_copies(tile, make_copy):
    def body(group, carry):
        for row in range(SUBLANES):
            for j in range(2):
                make_copy(group * SUBLANES + row, j).start(priority=row % 2)
        return carry

    lax.fori_loop(0, tile // SUBLANES, body, 0)


CAST_PARTS = 8


def _dispatch_kernel(pad_start_ref, pad_len_ref, pos_ref, x_ref, wgu_ref, wd_ref, xs_ref, wgu_bf_ref, wd_bf_ref,
                     rows_ref, zero_ref, sem, zsem, *, tile, tm):
    first = pl.program_id(0) == 0
    wgu_bf_ref[...] = wgu_ref[...].astype(BF16)
    wd_bf_ref[...] = wd_ref[...].astype(BF16)

    def pad_copies(action):
        for e in range(N_EXPERTS):
            start = pad_start_ref[e]
            run = tm // 2
            while run >= 1:
                take = (pad_len_ref[e] & run) != 0

                @pl.when(take)
                def _():
                    action(pltpu.make_async_copy(_slab(zero_ref, 0, run), _slab(xs_ref, start, run), zsem))

                start = start + jnp.where(take, run, 0)
                run //= 2
        n_tiles = xs_ref.shape[0] // (tm * SLAB)
        for t in range(n_tiles - (N_EXPERTS - 1), n_tiles):
            @pl.when(t >= pad_len_ref[N_EXPERTS])
            def _():
                for half in range(2):
                    action(pltpu.make_async_copy(zero_ref, _slab(xs_ref, t * tm + half * (tm // 2), tm // 2), zsem))

    @pl.when(first)
    def _():
        zero_ref[...] = jnp.zeros_like(zero_ref)
        pad_copies(lambda c: c.start())

    _to_slabs(rows_ref, x_ref[...])
    _start_row_copies(tile, lambda i, j: pltpu.make_async_copy(
        _slab(rows_ref, i), _slab(xs_ref, pos_ref[0, j, i]), sem.at[j]))
    for j in range(2):
        pltpu.make_async_copy(_slab(xs_ref, 0, tile), _slab(xs_ref, 0, tile), sem.at[j]).wait()

    @pl.when(first)
    def _():
        pad_copies(lambda c: c.wait())


def _dispatch(pad_start, pad_len, pos3, x_all, w_gu, w_down, l, n_rows, tile, tm):
    n = x_all.shape[0]
    steps = n // tile
    assert steps >= N_EXPERTS * CAST_PARTS
    m = l // 2
    gu_rows, d_rows = D_MODEL // CAST_PARTS, D_FF // CAST_PARTS

    def part(i):
        c = jnp.minimum(i, N_EXPERTS * CAST_PARTS - 1)
        return c // CAST_PARTS, c % CAST_PARTS

    grid_spec = pltpu.PrefetchScalarGridSpec(
        num_scalar_prefetch=2,
        grid=(steps,),
        in_specs=[
            pl.BlockSpec((1, 2, tile), lambda i, ps, pn: (i, 0, 0), memory_space=pltpu.SMEM),
            pl.BlockSpec((tile, D_MODEL), lambda i, ps, pn: (i, 0)),
            pl.BlockSpec((None, None, gu_rows, 2 * D_FF), lambda i, ps, pn: (m, *part(i), 0)),
            pl.BlockSpec((None, None, d_rows, D_MODEL), lambda i, ps, pn: (m, *part(i), 0)),
        ],
        out_specs=[
            pl.BlockSpec(memory_space=pl.ANY),
            pl.BlockSpec((None, gu_rows, 2 * D_FF), lambda i, ps, pn: (*part(i), 0)),
            pl.BlockSpec((None, d_rows, D_MODEL), lambda i, ps, pn: (*part(i), 0)),
        ],
        scratch_shapes=[pltpu.VMEM((tile * SLAB, LANES), F32), pltpu.VMEM((tm // 2 * SLAB, LANES), F32),
                        pltpu.SemaphoreType.DMA((2,)), pltpu.SemaphoreType.DMA(())],
    )
    return pl.pallas_call(
        functools.partial(_dispatch_kernel, tile=tile, tm=tm),
        grid_spec=grid_spec,
        out_shape=[
            jax.ShapeDtypeStruct((n_rows * SLAB, LANES), F32),
            jax.ShapeDtypeStruct((N_EXPERTS, D_MODEL, 2 * D_FF), BF16),
            jax.ShapeDtypeStruct((N_EXPERTS, D_FF, D_MODEL), BF16),
        ],
        compiler_params=_params("arbitrary"),
        name="moe_dispatch",
    )(pad_start, pad_len, pos3, x_all, w_gu, w_down)


def _moe_kernel(te_ref, used_ref, xs_ref, g_ref, wgu_ref, wd_ref, y_ref, act_ref, *, tm):
    used = pl.program_id(0) < used_ref[0]

    @pl.when(used)
    def _():
        _to_slabs(y_ref, _swiglu(_from_slabs(xs_ref, tm), g_ref, wgu_ref, wd_ref, act_ref))

    @pl.when(jnp.logical_not(used))
    def _():
        y_ref[...] = jnp.zeros_like(y_ref)


def _moe(tile_expert, n_used, xs, g_ffn3, wgu_bf, wd_bf, l, tm):
    nt = xs.shape[0] // (tm * SLAB)

    def tile_idx(t, used):
        return jnp.minimum(t, used[0] - 1)

    grid_spec = pltpu.PrefetchScalarGridSpec(
        num_scalar_prefetch=2,
        grid=(nt,),
        in_specs=[
            pl.BlockSpec((tm * SLAB, LANES), lambda t, te, used: (tile_idx(t, used), 0)),
            pl.BlockSpec((None, 1, D_MODEL), lambda t, te, used: (l, 0, 0)),
            pl.BlockSpec((None, D_MODEL, 2 * D_FF), lambda t, te, used: (te[tile_idx(t, used)], 0, 0)),
            pl.BlockSpec((None, D_FF, D_MODEL), lambda t, te, used: (te[tile_idx(t, used)], 0, 0)),
        ],
        out_specs=pl.BlockSpec((tm * SLAB, LANES), lambda t, te, used: (t, 0)),
        scratch_shapes=[pltpu.VMEM((tm, D_FF), BF16)],
    )
    return pl.pallas_call(
        functools.partial(_moe_kernel, tm=tm),
        grid_spec=grid_spec,
        out_shape=jax.ShapeDtypeStruct(xs.shape, F32),
        compiler_params=_params("arbitrary"),
        name="moe_experts",
    )(tile_expert, n_used, xs, g_ffn3, wgu_bf, wd_bf)


def _gather_combine(pos_ref, info_ref, x_ref, y_ref, buf_ref, sem, tile):
    _start_row_copies(tile, lambda i, j: pltpu.make_async_copy(
        _slab(y_ref, pos_ref[0, j, i]), _slab(buf_ref.at[j], i), sem.at[j]))
    for j in range(2):
        pltpu.make_async_copy(_slab(y_ref, 0, tile), _slab(y_ref, 0, tile), sem.at[j]).wait()
    info = info_ref[...]
    return (x_ref[...]
            + info[:, INFO_G:INFO_G + 1] * _from_slabs(buf_ref.at[0], tile)
            + info[:, INFO_G + 1:INFO_G + 2] * _from_slabs(buf_ref.at[1], tile))


def _combine_kernel(pos_ref, info_ref, x_ref, y_ref, o_ref, buf_ref, sem, *, tile):
    o_ref[...] = _gather_combine(pos_ref, info_ref, x_ref, y_ref, buf_ref, sem, tile)


def _combine_final_kernel(pos_ref, info_ref, x_ref, y_ref, g_ref, op_ref, os_ref, buf_ref, sem, *, tile, steps_p):
    out = _rms(_gather_combine(pos_ref, info_ref, x_ref, y_ref, buf_ref, sem, tile), g_ref[...])
    is_prompt = pl.program_id(0) < steps_p

    @pl.when(is_prompt)
    def _():
        op_ref[...] = out

    @pl.when(jnp.logical_not(is_prompt))
    def _():
        os_ref[...] = out


def _combine(pos3, info, x_all, y, tile, final=None):
    n = x_all.shape[0]
    row = lambda i: (i, 0)
    in_specs = [
        pl.BlockSpec((1, 2, tile), lambda i: (i, 0, 0), memory_space=pltpu.SMEM),
        pl.BlockSpec((tile, LANES), row),
        pl.BlockSpec((tile, D_MODEL), row),
        pl.BlockSpec(memory_space=pl.ANY),
    ]
    scratch = [pltpu.VMEM((2, tile * SLAB, LANES), F32), pltpu.SemaphoreType.DMA((2,))]
    if final is None:
        return pl.pallas_call(
            functools.partial(_combine_kernel, tile=tile),
            grid=(n // tile,),
            in_specs=in_specs,
            out_specs=pl.BlockSpec((tile, D_MODEL), row),
            out_shape=jax.ShapeDtypeStruct(x_all.shape, F32),
            scratch_shapes=scratch,
            input_output_aliases={2: 0},
            compiler_params=_params("arbitrary"),
            name="moe_combine",
        )(pos3, info, x_all, y)
    g_fin, n_p = final
    steps_p = n_p // tile
    return pl.pallas_call(
        functools.partial(_combine_final_kernel, tile=tile, steps_p=steps_p),
        grid=(n // tile,),
        in_specs=in_specs + [pl.BlockSpec((1, D_MODEL), lambda i: (0, 0))],
        out_specs=[
            pl.BlockSpec((tile, D_MODEL), lambda i: (jnp.minimum(i, steps_p - 1), 0)),
            pl.BlockSpec((tile, D_MODEL), lambda i: (jnp.maximum(i - steps_p, 0), 0)),
        ],
        out_shape=[jax.ShapeDtypeStruct((n_p, D_MODEL), F32), jax.ShapeDtypeStruct((n - n_p, D_MODEL), F32)],
        scratch_shapes=scratch,
        compiler_params=_params("arbitrary"),
        name="moe_combine_final",
    )(pos3, info, x_all, y, g_fin)


def _ffn_moe(x_all, g_ffn3, wr_hl, w_moe_gu, w_moe_down, l, tile, tm, final=None):
    n = x_all.shape[0]
    td = next(t for t in (256, 128, 64, 32, 16, 8) if n % t == 0 and n // t >= N_EXPERTS * CAST_PARTS)
    info, cnt = _route(x_all, g_ffn3, wr_hl, l, tile)
    counts = cnt[0, :N_EXPERTS].astype(I32)
    tiles_per = (counts + tm - 1) // tm
    tile_end = jnp.cumsum(tiles_per)
    base = (tile_end - tiles_per) * tm
    experts = info[:, INFO_E:INFO_E + 2].astype(I32)
    ranks = info[:, INFO_R:INFO_R + 2].astype(I32)
    pos = jnp.sum(jnp.where(experts[..., None] == jnp.arange(N_EXPERTS), base, 0), -1) + ranks
    n_tiles = (2 * n + N_EXPERTS * (tm - 1)) // tm
    n_used = tile_end[-1:]
    tile_expert = jnp.minimum(jnp.sum(jnp.arange(n_tiles)[:, None] >= tile_end[None, :], -1), N_EXPERTS - 1)
    blocked = lambda t: pos.T.reshape(2, n // t, t).transpose(1, 0, 2)
    pad_len = jnp.concatenate([tiles_per * tm - counts, n_used])
    xs, wgu_bf, wd_bf = _dispatch(base + counts, pad_len, blocked(td), x_all, w_moe_gu, w_moe_down, l,
                                  n_tiles * tm, td, tm)
    y = _moe(tile_expert.astype(I32), n_used.astype(I32), xs, g_ffn3, wgu_bf, wd_bf, l, tm)
    return _combine(blocked(tile), info, x_all, y, tile, final)


def _final_kernel(x_ref, g_ref, o_ref):
    o_ref[...] = _rms(x_ref[...], g_ref[...])


def _final_norm(x_all, g2, row0, rows, tile):
    off = row0 // tile
    return pl.pallas_call(
        _final_kernel,
        grid=(rows // tile,),
        in_specs=[
            pl.BlockSpec((tile, D_MODEL), lambda i: (off + i, 0)),
            pl.BlockSpec((1, D_MODEL), lambda i: (0, 0)),
        ],
        out_specs=pl.BlockSpec((tile, D_MODEL), lambda i: (i, 0)),
        out_shape=jax.ShapeDtypeStruct((rows, D_MODEL), F32),
        compiler_params=_params("parallel"),
        name="final_norm",
    )(x_all, g2)


def _rope_tables(pos):
    half = HEAD_DIM // 2
    inv = ROPE_THETA ** (-jnp.arange(half, dtype=F32) / half)
    ang = pos.astype(F32)[:, None] * inv[None, :]
    cos = jnp.cos(ang)
    sin = jnp.sin(ang)
    reps = LANES // HEAD_DIM
    return jnp.tile(jnp.concatenate([cos, cos], -1), (1, reps)), jnp.tile(jnp.concatenate([-sin, sin], -1), (1, reps))


def kernel(x_prompt, x_sample, cache_conv, cache_swa_k, cache_swa_v, g_mix, w_in, conv_w, conv_b, conv_ln_g,
           conv_ln_b, attn_sinks, w_out, g_ffn, w_dense_gu, w_dense_down, w_router, w_moe_gu, w_moe_down, g_final):
    batch, seq, _ = x_prompt.shape
    db, steps, _ = x_sample.shape
    depth = g_mix.shape[0]
    hist = cache_swa_k.shape[2]
    n_p, n_s = batch * seq, db * steps
    n = n_p + n_s
    assert seq % BLOCK == 0 and seq >= CONV_WIDTH - 1 and steps <= hist and steps < CONV_WIDTH - 1
    assert hist == min(WINDOW, PAST_LEN)
    tile = _pick_tile((1024, 512, 256, 128), n_p, n_s)
    tq = _pick_tile((1024, 512, 128), seq)
    tm = _pick_tile((512, 256), 2 * n)
    sb_attn = _pick_tile((16, 8), db)
    assert db % SUBLANES == 0
    conv_wb = jnp.broadcast_to(conv_w[:, :, None, :], (depth, CONV_WIDTH, SUBLANES, D_CONV))

    x_all = jnp.concatenate([x_prompt.reshape(n_p, D_MODEL), x_sample.reshape(n_s, D_MODEL)], 0)
    pos_all = jnp.concatenate([jnp.tile(jnp.arange(seq, dtype=I32), batch),
                               jnp.tile(PAST_LEN + jnp.arange(steps, dtype=I32), db)])
    cos_t, sin_t = _rope_tables(pos_all)

    vec3 = lambda a: a.reshape(a.shape[0], 1, a.shape[1])
    g_mix3, g_ffn3, conv_b3, ln_g3, ln_b3 = map(vec3, (g_mix, g_ffn, conv_b, conv_ln_g, conv_ln_b))
    w_in_bf, w_out_bf = w_in.astype(BF16), w_out.astype(BF16)
    wgu_bf, wd_bf = w_dense_gu.astype(BF16), w_dense_down.astype(BF16)
    wr = jnp.pad(w_router, ((0, 0), (0, 0), (0, LANES - N_EXPERTS)))
    wr_hi = wr.astype(BF16)
    wr_hl = jnp.concatenate([wr_hi, (wr - wr_hi.astype(F32)).astype(BF16)], -1)
    cache_k4 = cache_swa_k.reshape(depth, db, hist, D_KV)
    cache_v4 = cache_swa_v.reshape(depth, db, hist, D_KV)

    def tail_rows(a, rows):
        return jnp.stack([a[(b + 1) * seq - rows:(b + 1) * seq] for b in range(batch)], 0)

    keep = min(WINDOW, seq)
    g_fin = g_final.reshape(1, D_MODEL)
    conv_p, k_p, v_p, u_new, k_new, v_new = [], [], [], [], [], []
    for l in range(depth):
        u, q, k, v = _in_proj(x_all, g_mix3, w_in_bf, cos_t, sin_t, l, tile)

        c_p = _conv_prompt(u, conv_wb, conv_b3, ln_g3, ln_b3, l, batch, seq, tq)
        x_all = _attn_out_prompt(attn_sinks, q, k, v, c_p, w_out_bf, x_all, l, batch, seq, tq)

        u_s = u[n_p:].reshape(db, steps, D_CONV)
        k_s = k[n_p:].reshape(db, steps, D_KV)
        v_s = v[n_p:].reshape(db, steps, D_KV)
        c_s = _conv_sample(cache_conv, u_s, conv_wb, conv_b3, ln_g3, ln_b3, l).reshape(n_s, D_CONV)
        a_s = _attn_sample(attn_sinks, q[n_p:].reshape(db, steps, D_ATTN), k_s, v_s, cache_k4, cache_v4,
                           l, sb_attn).reshape(n_s, D_ATTN)
        x_all = _out_proj(c_s, a_s, w_out_bf, x_all, l, n_p, tile)

        conv_p.append(tail_rows(u, CONV_WIDTH - 1))
        k_p.append(tail_rows(k, keep).reshape(batch, keep, N_KV_HEADS, HEAD_DIM))
        v_p.append(tail_rows(v, keep).reshape(batch, keep, N_KV_HEADS, HEAD_DIM))
        u_new.append(u_s)
        k_new.append(k_s.reshape(db, steps, N_KV_HEADS, HEAD_DIM))
        v_new.append(v_s.reshape(db, steps, N_KV_HEADS, HEAD_DIM))

        if l % 2 == 0:
            x_all = _ffn_dense(x_all, g_ffn3, wgu_bf, wd_bf, l, tile)
        elif l < depth - 1:
            x_all = _ffn_moe(x_all, g_ffn3, wr_hl, w_moe_gu, w_moe_down, l, tile, tm)
        else:
            y_p, y_s = _ffn_moe(x_all, g_ffn3, wr_hl, w_moe_gu, w_moe_down, l, tile, tm, final=(g_fin, n_p))

    if depth % 2:
        y_p, y_s = _final_norm(x_all, g_fin, 0, n_p, tile), _final_norm(x_all, g_fin, n_p, n_s, tile)
    y_prompt = y_p.reshape(batch, seq, D_MODEL)
    y_sample = y_s.reshape(db, steps, D_MODEL)
    state_s = lambda cache, new: jnp.concatenate([cache[:, :, steps:], jnp.stack(new, 0)], 2)
    return (y_prompt, y_sample, jnp.stack(conv_p, 0), jnp.stack(k_p, 0), jnp.stack(v_p, 0),
            state_s(cache_conv, u_new), state_s(cache_swa_k, k_new), state_s(cache_swa_v, v_new))
```

```python
import functools

import jax
import jax.numpy as jnp
from jax import lax
from jax.experimental import pallas as pl
from jax.experimental.pallas import tpu as pltpu

F32 = jnp.float32
BF16 = jnp.bfloat16
I32 = jnp.int32

D_MODEL = 1024
D_CONV = 512
N_HEADS = 8
HEAD_DIM = 64
N_KV_HEADS = 2
GROUP = N_HEADS // N_KV_HEADS
D_ATTN = N_HEADS * HEAD_DIM
D_KV = N_KV_HEADS * HEAD_DIM
D_IN = 2 * D_CONV + D_ATTN + 2 * D_KV
CONV_WIDTH = 31
WINDOW = 128
BLOCK = 128
ROPE_THETA = 10000.0
D_FF = 2816
N_EXPERTS = 8
EPS = 1e-6
PAST_LEN = 8192

LANES = 128
SUBLANES = 8
CONV_HALO = 32
CONV_PAD = CONV_HALO - (CONV_WIDTH - 1)
VMEM_LIMIT = 56 * 1024 * 1024


def _pick_tile(cands, *sizes):
    for c in cands:
        if all(s % c == 0 for s in sizes):
            return c
    raise ValueError(f"no tile in {cands} divides {sizes}")


def _params(*sem):
    return pltpu.CompilerParams(dimension_semantics=sem, vmem_limit_bytes=VMEM_LIMIT)


def _rms(x, g):
    return x * lax.rsqrt(jnp.mean(x * x, -1, keepdims=True) + EPS) * g


def _sigmoid(x):
    return 1.0 / (1.0 + jnp.exp(-x))


def _in_proj_kernel(x_ref, g_ref, w_ref, cos_ref, sin_ref, u_ref, q_ref, k_ref, v_ref):
    h = _rms(x_ref[...], g_ref[...]).astype(BF16)
    p = jnp.dot(h, w_ref[...], preferred_element_type=F32)
    u_ref[...] = p[:, :D_CONV] * _sigmoid(p[:, D_CONV:2 * D_CONV])
    cos = cos_ref[...]
    sin = sin_ref[...]
    lane = lax.broadcasted_iota(I32, cos.shape, 1)
    first_half = (lane % HEAD_DIM) < (HEAD_DIM // 2)

    def rope(xc):
        partner = jnp.where(first_half,
                            pltpu.roll(xc, LANES - HEAD_DIM // 2, 1),
                            pltpu.roll(xc, HEAD_DIM // 2, 1))
        return xc * cos + partner * sin

    q0 = 2 * D_CONV
    for c in range(D_ATTN // LANES):
        qc = rope(p[:, q0 + c * LANES:q0 + (c + 1) * LANES])
        q_ref[:, c * LANES:(c + 1) * LANES] = (qc * (HEAD_DIM ** -0.5)).astype(BF16)
    k0 = q0 + D_ATTN
    k_ref[...] = rope(p[:, k0:k0 + D_KV])
    v_ref[...] = p[:, k0 + D_KV:k0 + 2 * D_KV]


def _in_proj(x_all, g_mix3, w_in_bf, cos_t, sin_t, l, tile):
    n = x_all.shape[0]
    row = lambda i: (i, 0)
    return pl.pallas_call(
        _in_proj_kernel,
        grid=(n // tile,),
        in_specs=[
            pl.BlockSpec((tile, D_MODEL), row),
            pl.BlockSpec((None, 1, D_MODEL), lambda i: (l, 0, 0)),
            pl.BlockSpec((None, D_MODEL, D_IN), lambda i: (l, 0, 0)),
            pl.BlockSpec((tile, LANES), row),
            pl.BlockSpec((tile, LANES), row),
        ],
        out_specs=[
            pl.BlockSpec((tile, D_CONV), row),
            pl.BlockSpec((tile, D_ATTN), row),
            pl.BlockSpec((tile, D_KV), row),
            pl.BlockSpec((tile, D_KV), row),
        ],
        out_shape=[
            jax.ShapeDtypeStruct((n, D_CONV), F32),
            jax.ShapeDtypeStruct((n, D_ATTN), BF16),
            jax.ShapeDtypeStruct((n, D_KV), F32),
            jax.ShapeDtypeStruct((n, D_KV), F32),
        ],
        compiler_params=_params("parallel"),
        name="in_proj",
    )(x_all, g_mix3, w_in_bf, cos_t, sin_t)


def _ln_swish(y, g, b):
    mu = jnp.mean(y, -1, keepdims=True)
    yc = y - mu
    z = yc * lax.rsqrt(jnp.mean(yc * yc, -1, keepdims=True) + EPS) * g + b
    return z * _sigmoid(z)


def _conv_prompt_kernel(u_ref, wb_ref, cb_ref, lg_ref, lb_ref, c_ref, ext_ref, sh_ref, y_ref, *, tile, rb):
    @pl.when(pl.program_id(1) == 0)
    def _():
        ext_ref[0:CONV_HALO, :] = jnp.zeros((CONV_HALO, D_CONV), F32)

    ext_ref[CONV_HALO:CONV_HALO + tile, :] = u_ref[...]
    for b in range(1, SUBLANES):
        sh_ref[b - 1] = ext_ref[pl.ds(b, sh_ref.shape[1]), :]
    def row_block(i, carry):
        r0 = pl.multiple_of(i * rb, rb)
        accs = [jnp.broadcast_to(cb_ref[...], (SUBLANES, D_CONV))] * (rb // SUBLANES)
        for k in range(CONV_WIDTH):
            b = (CONV_PAD + k) % SUBLANES
            wk = wb_ref[k]
            for rg in range(rb // SUBLANES):
                row = pl.multiple_of(r0 + (CONV_PAD + k - b) + rg * SUBLANES, SUBLANES)
                src = ext_ref[pl.ds(row, SUBLANES), :] if b == 0 else sh_ref[b - 1, pl.ds(row, SUBLANES), :]
                accs[rg] = accs[rg] + src * wk
        for rg in range(rb // SUBLANES):
            y_ref[pl.ds(pl.multiple_of(r0 + rg * SUBLANES, SUBLANES), SUBLANES), :] = accs[rg]
        return carry

    lax.fori_loop(0, tile // rb, row_block, 0)
    c_ref[...] = _ln_swish(y_ref[...], lg_ref[...], lb_ref[...]).astype(BF16)
    ext_ref[0:CONV_HALO, :] = ext_ref[tile:tile + CONV_HALO, :]


def _conv_prompt(u_all, conv_wb, conv_b3, ln_g3, ln_b3, l, batch, seq, tile):
    nt = seq // tile
    vec = pl.BlockSpec((None, 1, D_CONV), lambda b, j: (l, 0, 0))
    return pl.pallas_call(
        functools.partial(_conv_prompt_kernel, tile=tile, rb=32),
        grid=(batch, nt),
        in_specs=[
            pl.BlockSpec((tile, D_CONV), lambda b, j: (b * nt + j, 0)),
            pl.BlockSpec((None, CONV_WIDTH, SUBLANES, D_CONV), lambda b, j: (l, 0, 0, 0)),
            vec, vec, vec,
        ],
        out_specs=pl.BlockSpec((tile, D_CONV), lambda b, j: (b * nt + j, 0)),
        out_shape=jax.ShapeDtypeStruct((batch * seq, D_CONV), BF16),
        scratch_shapes=[pltpu.VMEM((tile + CONV_HALO, D_CONV), F32),
                        pltpu.VMEM((SUBLANES - 1, tile + CONV_HALO - SUBLANES, D_CONV), F32),
                        pltpu.VMEM((tile, D_CONV), F32)],
        compiler_params=_params("arbitrary", "arbitrary"),
        name="conv_prompt",
    )(u_all, conv_wb, conv_b3, ln_g3, ln_b3)


def _conv_sample_kernel(hist_ref, u_ref, wb_ref, cb_ref, lg_ref, lb_ref, c_ref):
    sb, steps, _ = u_ref.shape
    hist = CONV_WIDTH - 1
    accs = [jnp.broadcast_to(cb_ref[...], (sb, D_CONV)) for _ in range(steps)]
    for r in range(hist + steps):
        row = hist_ref[:, r, :] if r < hist else u_ref[:, r - hist, :]
        for t in range(steps):
            if 0 <= r - t < CONV_WIDTH:
                accs[t] = accs[t] + row * wb_ref[r - t]
    for t in range(steps):
        c_ref[:, t, :] = _ln_swish(accs[t], lg_ref[...], lb_ref[...]).astype(BF16)


def _conv_sample(cache_conv, u_s, conv_wb, conv_b3, ln_g3, ln_b3, l):
    db, steps, _ = u_s.shape
    sb = SUBLANES
    vec = pl.BlockSpec((None, 1, D_CONV), lambda i: (l, 0, 0))
    return pl.pallas_call(
        _conv_sample_kernel,
        grid=(db // sb,),
        in_specs=[
            pl.BlockSpec((None, sb, CONV_WIDTH - 1, D_CONV), lambda i: (l, i, 0, 0)),
            pl.BlockSpec((sb, steps, D_CONV), lambda i: (i, 0, 0)),
            pl.BlockSpec((None, CONV_WIDTH, SUBLANES, D_CONV), lambda i: (l, 0, 0, 0)),
            vec, vec, vec,
        ],
        out_specs=pl.BlockSpec((sb, steps, D_CONV), lambda i: (i, 0, 0)),
        out_shape=jax.ShapeDtypeStruct((db, steps, D_CONV), BF16),
        compiler_params=_params("parallel"),
        name="conv_sample",
    )(cache_conv, u_s, conv_wb, conv_b3, ln_g3, ln_b3)


def _softmax_sink_pv(s, mask, sink, v_bf, dot_pv):
    s = jnp.where(mask, s, -jnp.inf)
    m = jnp.maximum(jnp.max(s, -1, keepdims=True), sink)
    p = jnp.exp(s - m)
    den = jnp.sum(p, -1, keepdims=True) + jnp.exp(sink - m)
    return dot_pv(p.astype(BF16), v_bf) / den


def _attn_prompt_kernel(sink_ref, q_ref, kp_ref, kc_ref, vp_ref, vc_ref, c_ref, w_ref, x_ref, xo_ref, o_ref,
                        *, l, tq):
    first = pl.program_id(1) == 0
    cols = GROUP * BLOCK
    kj = lax.broadcasted_iota(I32, (2 * BLOCK, cols), 0)
    qi = lax.broadcasted_iota(I32, (2 * BLOCK, cols), 1) % BLOCK
    band = (kj > qi) & (kj <= qi + BLOCK)
    head_of_col = lax.broadcasted_iota(I32, (1, cols), 1) // BLOCK
    for qb in range(tq // BLOCK):
        if qb == 0:
            kk = jnp.concatenate([kp_ref[...], kc_ref[0:BLOCK, :]], 0)
            vv = jnp.concatenate([vp_ref[...], vc_ref[0:BLOCK, :]], 0)
            mask = band & (kj >= jnp.where(first, BLOCK, 0))
        else:
            kk = kc_ref[(qb - 1) * BLOCK:(qb + 1) * BLOCK, :]
            vv = vc_ref[(qb - 1) * BLOCK:(qb + 1) * BLOCK, :]
            mask = band
        kk = kk.astype(BF16)
        q = q_ref[qb * BLOCK:(qb + 1) * BLOCK, :]
        for g in range(N_KV_HEADS):
            hs = [g * GROUP + i for i in range(GROUP)]
            q4 = jnp.concatenate([q[:, h * HEAD_DIM:(h + 1) * HEAD_DIM] for h in hs], 0)
            sink = jnp.zeros((1, cols), F32)
            for i, h in enumerate(hs):
                sink = jnp.where(head_of_col == i, sink_ref[l, h], sink)
            s = lax.dot_general(kk[:, g * HEAD_DIM:(g + 1) * HEAD_DIM], q4, (((1,), (1,)), ((), ())),
                                preferred_element_type=F32)
            s = jnp.where(mask, s, -jnp.inf)
            m = jnp.maximum(jnp.max(s, 0, keepdims=True), sink)
            p = jnp.exp(s - m)
            den = jnp.sum(p, 0, keepdims=True) + jnp.exp(sink - m)
            v_t = vv[:, g * HEAD_DIM:(g + 1) * HEAD_DIM].T.astype(BF16)
            o_t = jnp.dot(v_t, p.astype(BF16), preferred_element_type=F32) * (1.0 / den)
            for i, h in enumerate(hs):
                o_ref[qb * BLOCK:(qb + 1) * BLOCK, h * HEAD_DIM:(h + 1) * HEAD_DIM] = (
                    o_t[:, i * BLOCK:(i + 1) * BLOCK].T.astype(BF16))
        rows = slice(qb * BLOCK, (qb + 1) * BLOCK)
        xo_ref[rows, :] = (x_ref[rows, :]
                           + jnp.dot(c_ref[rows, :], w_ref[0:D_CONV, :], preferred_element_type=F32)
                           + jnp.dot(o_ref[rows, :], w_ref[D_CONV:, :], preferred_element_type=F32))


def _attn_out_prompt(sinks, q_all, k_all, v_all, c_p, w_out_bf, x_all, l, batch, seq, tq):
    nq = seq // tq
    per = tq // BLOCK
    cur = lambda b, j: (b * nq + j, 0)
    prev = lambda b, j: (jnp.maximum((b * nq + j) * per - 1, 0), 0)
    return pl.pallas_call(
        functools.partial(_attn_prompt_kernel, l=l, tq=tq),
        grid=(batch, nq),
        in_specs=[
            pl.BlockSpec(memory_space=pltpu.SMEM),
            pl.BlockSpec((tq, D_ATTN), cur),
            pl.BlockSpec((BLOCK, D_KV), prev),
            pl.BlockSpec((tq, D_KV), cur),
            pl.BlockSpec((BLOCK, D_KV), prev),
            pl.BlockSpec((tq, D_KV), cur),
            pl.BlockSpec((tq, D_CONV), cur),
            pl.BlockSpec((None, D_MODEL, D_MODEL), lambda b, j: (l, 0, 0)),
            pl.BlockSpec((tq, D_MODEL), cur),
        ],
        out_specs=pl.BlockSpec((tq, D_MODEL), cur),
        out_shape=jax.ShapeDtypeStruct(x_all.shape, F32),
        scratch_shapes=[pltpu.VMEM((tq, D_ATTN), BF16)],
        input_output_aliases={8: 0},
        compiler_params=_params("parallel", "parallel"),
        name="attn_out_prompt",
    )(sinks, q_all, k_all, k_all, v_all, v_all, c_p, w_out_bf, x_all)


def _attn_sample_kernel(sink_ref, q_ref, kn_ref, vn_ref, kc_ref, vc_ref, o_ref, *, l, steps):
    sb = q_ref.shape[0]
    hist = kc_ref.shape[1]
    rows = GROUP * steps
    pad = jnp.zeros((sb, hist - steps, D_KV), F32)
    kk = jnp.concatenate([kc_ref[...], kn_ref[...], pad], 1).astype(BF16)
    vv = jnp.concatenate([vc_ref[...], vn_ref[...], pad], 1).astype(BF16)
    t = lax.broadcasted_iota(I32, (sb, rows, 2 * hist), 1) % steps
    j = lax.broadcasted_iota(I32, (sb, rows, 2 * hist), 2)
    mask = ((j < hist) & (t + hist - j < WINDOW)) | ((j >= hist) & (j - hist <= t) & (t - (j - hist) < WINDOW))
    head_of_row = lax.broadcasted_iota(I32, (rows, 1), 0) // steps
    dot_qk = lambda a, b: jnp.einsum("bqd,bkd->bqk", a, b, preferred_element_type=F32)
    dot_pv = lambda a, b: jnp.einsum("bqk,bkd->bqd", a, b, preferred_element_type=F32)
    q = q_ref[...]
    for g in range(N_KV_HEADS):
        hs = [g * GROUP + i for i in range(GROUP)]
        q4 = jnp.concatenate([q[:, :, h * HEAD_DIM:(h + 1) * HEAD_DIM] for h in hs], 1)
        sink = jnp.zeros((rows, 1), F32)
        for i, h in enumerate(hs):
            sink = jnp.where(head_of_row == i, sink_ref[l, h], sink)
        s = dot_qk(q4, kk[:, :, g * HEAD_DIM:(g + 1) * HEAD_DIM])
        o = _softmax_sink_pv(s, mask, sink[None], vv[:, :, g * HEAD_DIM:(g + 1) * HEAD_DIM], dot_pv)
        for i, h in enumerate(hs):
            o_ref[:, :, h * HEAD_DIM:(h + 1) * HEAD_DIM] = o[:, i * steps:(i + 1) * steps, :].astype(BF16)


def _attn_sample(sinks, q_s, k_new, v_new, cache_k4, cache_v4, l, sb):
    db, steps, _ = q_s.shape
    hist = cache_k4.shape[2]
    seq3 = lambda i: (i, 0, 0)
    return pl.pallas_call(
        functools.partial(_attn_sample_kernel, l=l, steps=steps),
        grid=(db // sb,),
        in_specs=[
            pl.BlockSpec(memory_space=pltpu.SMEM),
            pl.BlockSpec((sb, steps, D_ATTN), seq3),
            pl.BlockSpec((sb, steps, D_KV), seq3),
            pl.BlockSpec((sb, steps, D_KV), seq3),
            pl.BlockSpec((None, sb, hist, D_KV), lambda i: (l, i, 0, 0)),
            pl.BlockSpec((None, sb, hist, D_KV), lambda i: (l, i, 0, 0)),
        ],
        out_specs=pl.BlockSpec((sb, steps, D_ATTN), seq3),
        out_shape=jax.ShapeDtypeStruct((db, steps, D_ATTN), BF16),
        compiler_params=_params("parallel"),
        name="attn_sample",
    )(sinks, q_s, k_new, v_new, cache_k4, cache_v4)


def _out_proj_kernel(c_ref, a_ref, w_ref, x_ref, o_ref):
    o_ref[...] = (x_ref[...]
                  + jnp.dot(c_ref[...], w_ref[0:D_CONV, :], preferred_element_type=F32)
                  + jnp.dot(a_ref[...], w_ref[D_CONV:, :], preferred_element_type=F32))


def _out_proj(c, a, w_out_bf, x_all, l, row0, tile):
    rows = c.shape[0]
    off = row0 // tile
    return pl.pallas_call(
        _out_proj_kernel,
        grid=(rows // tile,),
        in_specs=[
            pl.BlockSpec((tile, D_CONV), lambda i: (i, 0)),
            pl.BlockSpec((tile, D_ATTN), lambda i: (i, 0)),
            pl.BlockSpec((None, D_MODEL, D_MODEL), lambda i: (l, 0, 0)),
            pl.BlockSpec((tile, D_MODEL), lambda i: (off + i, 0)),
        ],
        out_specs=pl.BlockSpec((tile, D_MODEL), lambda i: (off + i, 0)),
        out_shape=jax.ShapeDtypeStruct(x_all.shape, F32),
        input_output_aliases={3: 0},
        compiler_params=_params("parallel"),
        name="out_proj",
    )(c, a, w_out_bf, x_all)


FF_CHUNK = 256


def _swiglu(x, g_ref, wgu_ref, wd_ref, act_ref):
    h = _rms(x, g_ref[...]).astype(BF16)
    for c0 in range(0, D_FF, FF_CHUNK):
        gate = jnp.dot(h, wgu_ref[:, c0:c0 + FF_CHUNK], preferred_element_type=F32)
        up = jnp.dot(h, wgu_ref[:, D_FF + c0:D_FF + c0 + FF_CHUNK], preferred_element_type=F32)
        act_ref[:, c0:c0 + FF_CHUNK] = (gate * _sigmoid(gate) * up).astype(BF16)
    return jnp.dot(act_ref[...], wd_ref[...], preferred_element_type=F32)


def _ffn_dense_kernel(x_ref, g_ref, wgu_ref, wd_ref, o_ref, act_ref):
    x = x_ref[...]
    o_ref[...] = x + _swiglu(x, g_ref, wgu_ref, wd_ref, act_ref)


def _ffn_dense(x_all, g_ffn3, wgu_bf, wd_bf, l, tile):
    n = x_all.shape[0]
    once = pl.Buffered(1)
    return pl.pallas_call(
        _ffn_dense_kernel,
        grid=(n // tile,),
        in_specs=[
            pl.BlockSpec((tile, D_MODEL), lambda i: (i, 0)),
            pl.BlockSpec((None, 1, D_MODEL), lambda i: (l, 0, 0)),
            pl.BlockSpec((None, D_MODEL, 2 * D_FF), lambda i: (l // 2, 0, 0), pipeline_mode=once),
            pl.BlockSpec((None, D_FF, D_MODEL), lambda i: (l // 2, 0, 0), pipeline_mode=once),
        ],
        out_specs=pl.BlockSpec((tile, D_MODEL), lambda i: (i, 0)),
        out_shape=jax.ShapeDtypeStruct(x_all.shape, F32),
        scratch_shapes=[pltpu.VMEM((tile, D_FF), BF16)],
        input_output_aliases={0: 0},
        compiler_params=_params("parallel"),
        name="ffn_dense",
    )(x_all, g_ffn3, wgu_bf, wd_bf)


INFO_E, INFO_G, INFO_R = 0, 2, 4


ROUTE_CAST_PARTS = 2


def _route_kernel(x_ref, g_ref, whl_ref, wd_ref, info_ref, cnt_ref, wd_bf_ref, run_ref):
    @pl.when(pl.program_id(0) == 0)
    def _():
        run_ref[...] = jnp.zeros_like(run_ref)

    wd_bf_ref[...] = wd_ref[...].astype(BF16)

    h = _rms(x_ref[...], g_ref[...])
    h_hi = h.astype(BF16)
    h_lo = (h - h_hi.astype(F32)).astype(BF16)
    dot = lambda a, b: jnp.dot(a, b, preferred_element_type=F32)
    hh = dot(h_hi, whl_ref[...])
    logits = hh[:, :LANES] + hh[:, LANES:] + dot(h_lo, whl_ref[:, :LANES])
    t = logits.shape[0]
    lane = lax.broadcasted_iota(I32, logits.shape, 1)
    lg = jnp.where(lane < N_EXPERTS, logits, -jnp.inf)
    m1 = jnp.max(lg, -1, keepdims=True)
    i1 = jnp.min(jnp.where(lg == m1, lane, LANES), -1, keepdims=True)
    lg2 = jnp.where(lane == i1, -jnp.inf, lg)
    m2 = jnp.max(lg2, -1, keepdims=True)
    i2 = jnp.min(jnp.where(lg2 == m2, lane, LANES), -1, keepdims=True)
    e = jnp.exp(m2 - m1)
    g1 = 1.0 / (1.0 + e)
    g2 = e / (1.0 + e)
    sel1 = lane == i1
    sel2 = lane == i2
    onehot = jnp.where(sel1 | sel2, 1.0, 0.0)
    r = lax.broadcasted_iota(I32, (t, t), 0)
    c = lax.broadcasted_iota(I32, (t, t), 1)
    tri = jnp.where(r > c, 1.0, 0.0).astype(BF16)
    before = dot(tri, onehot.astype(BF16)) + run_ref[0:1, :]
    r1 = jnp.sum(jnp.where(sel1, before, 0.0), -1, keepdims=True)
    r2 = jnp.sum(jnp.where(sel2, before, 0.0), -1, keepdims=True)
    run_ref[...] = run_ref[...] + jnp.sum(onehot, 0, keepdims=True)
    cnt_ref[...] = run_ref[...]
    info = jnp.zeros(logits.shape, F32)
    for pos, val in ((INFO_E, i1.astype(F32)), (INFO_E + 1, i2.astype(F32)), (INFO_G, g1), (INFO_G + 1, g2),
                     (INFO_R, r1), (INFO_R + 1, r2)):
        info = jnp.where(lane == pos, val, info)
    info_ref[...] = info


def _route(x_all, g_ffn3, wr_hl, w_down, l, tile):
    n = x_all.shape[0]
    steps = n // tile
    assert steps >= N_EXPERTS * ROUTE_CAST_PARTS
    m = l // 2
    d_rows = D_FF // ROUTE_CAST_PARTS

    def part(i):
        c = jnp.minimum(i, N_EXPERTS * ROUTE_CAST_PARTS - 1)
        return c // ROUTE_CAST_PARTS, c % ROUTE_CAST_PARTS

    return pl.pallas_call(
        _route_kernel,
        grid=(steps,),
        in_specs=[
            pl.BlockSpec((tile, D_MODEL), lambda i: (i, 0)),
            pl.BlockSpec((None, 1, D_MODEL), lambda i: (l, 0, 0)),
            pl.BlockSpec((None, D_MODEL, 2 * LANES), lambda i: (m, 0, 0)),
            pl.BlockSpec((None, None, d_rows, D_MODEL), lambda i: (m, *part(i), 0)),
        ],
        out_specs=[
            pl.BlockSpec((tile, LANES), lambda i: (i, 0)),
            pl.BlockSpec((SUBLANES, LANES), lambda i: (0, 0)),
            pl.BlockSpec((None, d_rows, D_MODEL), lambda i: (*part(i), 0)),
        ],
        out_shape=[
            jax.ShapeDtypeStruct((n, LANES), F32),
            jax.ShapeDtypeStruct((SUBLANES, LANES), F32),
            jax.ShapeDtypeStruct((N_EXPERTS, D_FF, D_MODEL), BF16),
        ],
        scratch_shapes=[pltpu.VMEM((SUBLANES, LANES), F32)],
        compiler_params=_params("arbitrary"),
        name="route",
    )(x_all, g_ffn3, wr_hl, w_down)


SLAB = D_MODEL // LANES
assert SLAB == SUBLANES


def _slab(ref, row, rows=1):
    start = row * SLAB
    if not isinstance(start, int):
        start = pl.multiple_of(start, SLAB)
    return ref.at[pl.ds(start, rows * SLAB)]


def _to_slabs(slab_ref, x):
    for s in range(SLAB):
        slab_ref[pl.ds(s, x.shape[0], stride=SLAB), :] = x[:, s * LANES:(s + 1) * LANES]


def _from_slabs(slab_ref, rows):
    return jnp.concatenate([slab_ref[pl.ds(s, rows, stride=SLAB), :] for s in range(SLAB)], 1)


def _start_row_copies(tile, make_copy):
    def body(group, carry):
        for row in range(SUBLANES):
            for j in range(2):
                make_copy(group * SUBLANES + row, j).start(priority=row % 2)
        return carry

    lax.fori_loop(0, tile // SUBLANES, body, 0)


CAST_PARTS = 8


def _dispatch_kernel(pad_start_ref, pad_len_ref, pos_ref, x_ref, wgu_ref, xs_ref, wgu_bf_ref,
                     rows_ref, zero_ref, sem, zsem, *, tile, tm):
    first = pl.program_id(0) == 0
    wgu_bf_ref[...] = wgu_ref[...].astype(BF16)

    def pad_copies(action):
        for e in range(N_EXPERTS):
            start = pad_start_ref[e]
            run = tm // 2
            while run >= 1:
                take = (pad_len_ref[e] & run) != 0

                @pl.when(take)
                def _():
                    action(pltpu.make_async_copy(_slab(zero_ref, 0, run), _slab(xs_ref, start, run), zsem))

                start = start + jnp.where(take, run, 0)
                run //= 2
        n_tiles = xs_ref.shape[0] // (tm * SLAB)
        for t in range(n_tiles - (N_EXPERTS - 1), n_tiles):
            @pl.when(t >= pad_len_ref[N_EXPERTS])
            def _():
                for half in range(2):
                    action(pltpu.make_async_copy(zero_ref, _slab(xs_ref, t * tm + half * (tm // 2), tm // 2), zsem))

    @pl.when(first)
    def _():
        zero_ref[...] = jnp.zeros_like(zero_ref)
        pad_copies(lambda c: c.start())

    _to_slabs(rows_ref, x_ref[...])
    _start_row_copies(tile, lambda i, j: pltpu.make_async_copy(
        _slab(rows_ref, i), _slab(xs_ref, pos_ref[0, j, i]), sem.at[j]))
    for j in range(2):
        pltpu.make_async_copy(_slab(xs_ref, 0, tile), _slab(xs_ref, 0, tile), sem.at[j]).wait()

    @pl.when(first)
    def _():
        pad_copies(lambda c: c.wait())


def _dispatch(pad_start, pad_len, pos3, x_all, w_gu, l, n_rows, tile, tm):
    n = x_all.shape[0]
    steps = n // tile
    assert steps >= N_EXPERTS * CAST_PARTS
    m = l // 2
    gu_rows = D_MODEL // CAST_PARTS

    def part(i):
        c = jnp.minimum(i, N_EXPERTS * CAST_PARTS - 1)
        return c // CAST_PARTS, c % CAST_PARTS

    grid_spec = pltpu.PrefetchScalarGridSpec(
        num_scalar_prefetch=2,
        grid=(steps,),
        in_specs=[
            pl.BlockSpec((1, 2, tile), lambda i, ps, pn: (i, 0, 0), memory_space=pltpu.SMEM),
            pl.BlockSpec((tile, D_MODEL), lambda i, ps, pn: (i, 0)),
            pl.BlockSpec((None, None, gu_rows, 2 * D_FF), lambda i, ps, pn: (m, *part(i), 0)),
        ],
        out_specs=[
            pl.BlockSpec(memory_space=pl.ANY),
            pl.BlockSpec((None, gu_rows, 2 * D_FF), lambda i, ps, pn: (*part(i), 0)),
        ],
        scratch_shapes=[pltpu.VMEM((tile * SLAB, LANES), F32), pltpu.VMEM((tm // 2 * SLAB, LANES), F32),
                        pltpu.SemaphoreType.DMA((2,)), pltpu.SemaphoreType.DMA(())],
    )
    return pl.pallas_call(
        functools.partial(_dispatch_kernel, tile=tile, tm=tm),
        grid_spec=grid_spec,
        out_shape=[
            jax.ShapeDtypeStruct((n_rows * SLAB, LANES), F32),
            jax.ShapeDtypeStruct((N_EXPERTS, D_MODEL, 2 * D_FF), BF16),
        ],
        compiler_params=_params("arbitrary"),
        name="moe_dispatch",
    )(pad_start, pad_len, pos3, x_all, w_gu)


def _moe_kernel(te_ref, used_ref, xs_ref, g_ref, wgu_ref, wd_ref, y_ref, act_ref, *, tm):
    used = pl.program_id(0) < used_ref[0]

    @pl.when(used)
    def _():
        _to_slabs(y_ref, _swiglu(_from_slabs(xs_ref, tm), g_ref, wgu_ref, wd_ref, act_ref))

    @pl.when(jnp.logical_not(used))
    def _():
        y_ref[...] = jnp.zeros_like(y_ref)


def _moe(tile_expert, n_used, xs, g_ffn3, wgu_bf, wd_bf, l, tm):
    nt = xs.shape[0] // (tm * SLAB)

    def tile_idx(t, used):
        return jnp.minimum(t, used[0] - 1)

    grid_spec = pltpu.PrefetchScalarGridSpec(
        num_scalar_prefetch=2,
        grid=(nt,),
        in_specs=[
            pl.BlockSpec((tm * SLAB, LANES), lambda t, te, used: (tile_idx(t, used), 0)),
            pl.BlockSpec((None, 1, D_MODEL), lambda t, te, used: (l, 0, 0)),
            pl.BlockSpec((None, D_MODEL, 2 * D_FF), lambda t, te, used: (te[tile_idx(t, used)], 0, 0)),
            pl.BlockSpec((None, D_FF, D_MODEL), lambda t, te, used: (te[tile_idx(t, used)], 0, 0)),
        ],
        out_specs=pl.BlockSpec((tm * SLAB, LANES), lambda t, te, used: (t, 0)),
        scratch_shapes=[pltpu.VMEM((tm, D_FF), BF16)],
    )
    return pl.pallas_call(
        functools.partial(_moe_kernel, tm=tm),
        grid_spec=grid_spec,
        out_shape=jax.ShapeDtypeStruct(xs.shape, F32),
        compiler_params=_params("arbitrary"),
        name="moe_experts",
    )(tile_expert, n_used, xs, g_ffn3, wgu_bf, wd_bf)


def _gather_combine(pos_ref, info_ref, x_ref, y_ref, buf_ref, sem, tile):
    _start_row_copies(tile, lambda i, j: pltpu.make_async_copy(
        _slab(y_ref, pos_ref[0, j, i]), _slab(buf_ref.at[j], i), sem.at[j]))
    for j in range(2):
        pltpu.make_async_copy(_slab(y_ref, 0, tile), _slab(y_ref, 0, tile), sem.at[j]).wait()
    info = info_ref[...]
    return (x_ref[...]
            + info[:, INFO_G:INFO_G + 1] * _from_slabs(buf_ref.at[0], tile)
            + info[:, INFO_G + 1:INFO_G + 2] * _from_slabs(buf_ref.at[1], tile))


def _combine_kernel(pos_ref, info_ref, x_ref, y_ref, o_ref, buf_ref, sem, *, tile):
    o_ref[...] = _gather_combine(pos_ref, info_ref, x_ref, y_ref, buf_ref, sem, tile)


def _combine_final_kernel(pos_ref, info_ref, x_ref, y_ref, g_ref, op_ref, os_ref, buf_ref, sem, *, tile, steps_p):
    out = _rms(_gather_combine(pos_ref, info_ref, x_ref, y_ref, buf_ref, sem, tile), g_ref[...])
    is_prompt = pl.program_id(0) < steps_p

    @pl.when(is_prompt)
    def _():
        op_ref[...] = out

    @pl.when(jnp.logical_not(is_prompt))
    def _():
        os_ref[...] = out


def _combine(pos3, info, x_all, y, tile, final=None):
    n = x_all.shape[0]
    row = lambda i: (i, 0)
    in_specs = [
        pl.BlockSpec((1, 2, tile), lambda i: (i, 0, 0), memory_space=pltpu.SMEM),
        pl.BlockSpec((tile, LANES), row),
        pl.BlockSpec((tile, D_MODEL), row),
        pl.BlockSpec(memory_space=pl.ANY),
    ]
    scratch = [pltpu.VMEM((2, tile * SLAB, LANES), F32), pltpu.SemaphoreType.DMA((2,))]
    if final is None:
        return pl.pallas_call(
            functools.partial(_combine_kernel, tile=tile),
            grid=(n // tile,),
            in_specs=in_specs,
            out_specs=pl.BlockSpec((tile, D_MODEL), row),
            out_shape=jax.ShapeDtypeStruct(x_all.shape, F32),
            scratch_shapes=scratch,
            input_output_aliases={2: 0},
            compiler_params=_params("arbitrary"),
            name="moe_combine",
        )(pos3, info, x_all, y)
    g_fin, n_p = final
    steps_p = n_p // tile
    return pl.pallas_call(
        functools.partial(_combine_final_kernel, tile=tile, steps_p=steps_p),
        grid=(n // tile,),
        in_specs=in_specs + [pl.BlockSpec((1, D_MODEL), lambda i: (0, 0))],
        out_specs=[
            pl.BlockSpec((tile, D_MODEL), lambda i: (jnp.minimum(i, steps_p - 1), 0)),
            pl.BlockSpec((tile, D_MODEL), lambda i: (jnp.maximum(i - steps_p, 0), 0)),
        ],
        out_shape=[jax.ShapeDtypeStruct((n_p, D_MODEL), F32), jax.ShapeDtypeStruct((n - n_p, D_MODEL), F32)],
        scratch_shapes=scratch,
        compiler_params=_params("arbitrary"),
        name="moe_combine_final",
    )(pos3, info, x_all, y, g_fin)


def _ffn_moe(x_all, g_ffn3, wr_hl, w_moe_gu, w_moe_down, l, tile, tm, final=None):
    n = x_all.shape[0]
    td = next(t for t in (256, 128, 64, 32, 16, 8) if n % t == 0 and n // t >= N_EXPERTS * CAST_PARTS)
    info, cnt, wd_bf = _route(x_all, g_ffn3, wr_hl, w_moe_down, l, tile)
    counts = cnt[0, :N_EXPERTS].astype(I32)
    tiles_per = (counts + tm - 1) // tm
    tile_end = jnp.cumsum(tiles_per)
    base = (tile_end - tiles_per) * tm
    experts = info[:, INFO_E:INFO_E + 2].astype(I32)
    ranks = info[:, INFO_R:INFO_R + 2].astype(I32)
    pos = jnp.sum(jnp.where(experts[..., None] == jnp.arange(N_EXPERTS), base, 0), -1) + ranks
    n_tiles = (2 * n + N_EXPERTS * (tm - 1)) // tm
    n_used = tile_end[-1:]
    tile_expert = jnp.minimum(jnp.sum(jnp.arange(n_tiles)[:, None] >= tile_end[None, :], -1), N_EXPERTS - 1)
    blocked = lambda t: pos.T.reshape(2, n // t, t).transpose(1, 0, 2)
    pad_len = jnp.concatenate([tiles_per * tm - counts, n_used])
    xs, wgu_bf = _dispatch(base + counts, pad_len, blocked(td), x_all, w_moe_gu, l, n_tiles * tm, td, tm)
    y = _moe(tile_expert.astype(I32), n_used.astype(I32), xs, g_ffn3, wgu_bf, wd_bf, l, tm)
    return _combine(blocked(tile), info, x_all, y, tile, final)


def _final_kernel(x_ref, g_ref, o_ref):
    o_ref[...] = _rms(x_ref[...], g_ref[...])


def _final_norm(x_all, g2, row0, rows, tile):
    off = row0 // tile
    return pl.pallas_call(
        _final_kernel,
        grid=(rows // tile,),
        in_specs=[
            pl.BlockSpec((tile, D_MODEL), lambda i: (off + i, 0)),
            pl.BlockSpec((1, D_MODEL), lambda i: (0, 0)),
        ],
        out_specs=pl.BlockSpec((tile, D_MODEL), lambda i: (i, 0)),
        out_shape=jax.ShapeDtypeStruct((rows, D_MODEL), F32),
        compiler_params=_params("parallel"),
        name="final_norm",
    )(x_all, g2)


def _rope_tables(pos):
    half = HEAD_DIM // 2
    inv = ROPE_THETA ** (-jnp.arange(half, dtype=F32) / half)
    ang = pos.astype(F32)[:, None] * inv[None, :]
    cos = jnp.cos(ang)
    sin = jnp.sin(ang)
    reps = LANES // HEAD_DIM
    return jnp.tile(jnp.concatenate([cos, cos], -1), (1, reps)), jnp.tile(jnp.concatenate([-sin, sin], -1), (1, reps))


def kernel(x_prompt, x_sample, cache_conv, cache_swa_k, cache_swa_v, g_mix, w_in, conv_w, conv_b, conv_ln_g,
           conv_ln_b, attn_sinks, w_out, g_ffn, w_dense_gu, w_dense_down, w_router, w_moe_gu, w_moe_down, g_final):
    batch, seq, _ = x_prompt.shape
    db, steps, _ = x_sample.shape
    depth = g_mix.shape[0]
    hist = cache_swa_k.shape[2]
    n_p, n_s = batch * seq, db * steps
    n = n_p + n_s
    assert seq % BLOCK == 0 and seq >= CONV_WIDTH - 1 and steps <= hist and steps < CONV_WIDTH - 1
    assert hist == min(WINDOW, PAST_LEN)
    tile = _pick_tile((1024, 512, 256, 128), n_p, n_s)
    tq = _pick_tile((1024, 512, 128), seq)
    tm = _pick_tile((512, 256), 2 * n)
    sb_attn = _pick_tile((16, 8), db)
    assert db % SUBLANES == 0
    conv_wb = jnp.broadcast_to(conv_w[:, :, None, :], (depth, CONV_WIDTH, SUBLANES, D_CONV))

    x_all = jnp.concatenate([x_prompt.reshape(n_p, D_MODEL), x_sample.reshape(n_s, D_MODEL)], 0)
    pos_all = jnp.concatenate([jnp.tile(jnp.arange(seq, dtype=I32), batch),
                               jnp.tile(PAST_LEN + jnp.arange(steps, dtype=I32), db)])
    cos_t, sin_t = _rope_tables(pos_all)

    vec3 = lambda a: a.reshape(a.shape[0], 1, a.shape[1])
    g_mix3, g_ffn3, conv_b3, ln_g3, ln_b3 = map(vec3, (g_mix, g_ffn, conv_b, conv_ln_g, conv_ln_b))
    w_in_bf, w_out_bf = w_in.astype(BF16), w_out.astype(BF16)
    wgu_bf, wd_bf = w_dense_gu.astype(BF16), w_dense_down.astype(BF16)
    wr = jnp.pad(w_router, ((0, 0), (0, 0), (0, LANES - N_EXPERTS)))
    wr_hi = wr.astype(BF16)
    wr_hl = jnp.concatenate([wr_hi, (wr - wr_hi.astype(F32)).astype(BF16)], -1)
    cache_k4 = cache_swa_k.reshape(depth, db, hist, D_KV)
    cache_v4 = cache_swa_v.reshape(depth, db, hist, D_KV)

    def tail_rows(a, rows):
        return jnp.stack([a[(b + 1) * seq - rows:(b + 1) * seq] for b in range(batch)], 0)

    keep = min(WINDOW, seq)
    g_fin = g_final.reshape(1, D_MODEL)
    conv_p, k_p, v_p, u_new, k_new, v_new = [], [], [], [], [], []
    for l in range(depth):
        u, q, k, v = _in_proj(x_all, g_mix3, w_in_bf, cos_t, sin_t, l, tile)

        c_p = _conv_prompt(u, conv_wb, conv_b3, ln_g3, ln_b3, l, batch, seq, tq)
        x_all = _attn_out_prompt(attn_sinks, q, k, v, c_p, w_out_bf, x_all, l, batch, seq, tq)

        u_s = u[n_p:].reshape(db, steps, D_CONV)
        k_s = k[n_p:].reshape(db, steps, D_KV)
        v_s = v[n_p:].reshape(db, steps, D_KV)
        c_s = _conv_sample(cache_conv, u_s, conv_wb, conv_b3, ln_g3, ln_b3, l).reshape(n_s, D_CONV)
        a_s = _attn_sample(attn_sinks, q[n_p:].reshape(db, steps, D_ATTN), k_s, v_s, cache_k4, cache_v4,
                           l, sb_attn).reshape(n_s, D_ATTN)
        x_all = _out_proj(c_s, a_s, w_out_bf, x_all, l, n_p, tile)

        conv_p.append(tail_rows(u, CONV_WIDTH - 1))
        k_p.append(tail_rows(k, keep).reshape(batch, keep, N_KV_HEADS, HEAD_DIM))
        v_p.append(tail_rows(v, keep).reshape(batch, keep, N_KV_HEADS, HEAD_DIM))
        u_new.append(u_s)
        k_new.append(k_s.reshape(db, steps, N_KV_HEADS, HEAD_DIM))
        v_new.append(v_s.reshape(db, steps, N_KV_HEADS, HEAD_DIM))

        if l % 2 == 0:
            x_all = _ffn_dense(x_all, g_ffn3, wgu_bf, wd_bf, l, tile)
        elif l < depth - 1:
            x_all = _ffn_moe(x_all, g_ffn3, wr_hl, w_moe_gu, w_moe_down, l, tile, tm)
        else:
            y_p, y_s = _ffn_moe(x_all, g_ffn3, wr_hl, w_moe_gu, w_moe_down, l, tile, tm, final=(g_fin, n_p))

    if depth % 2:
        y_p, y_s = _final_norm(x_all, g_fin, 0, n_p, tile), _final_norm(x_all, g_fin, n_p, n_s, tile)
    y_prompt = y_p.reshape(batch, seq, D_MODEL)
    y_sample = y_s.reshape(db, steps, D_MODEL)
    state_s = lambda cache, new: jnp.concatenate([cache[:, :, steps:], jnp.stack(new, 0)], 2)
    return (y_prompt, y_sample, jnp.stack(conv_p, 0), jnp.stack(k_p, 0), jnp.stack(v_p, 0),
            state_s(cache_conv, u_new), state_s(cache_swa_k, k_new), state_s(cache_swa_v, v_new))
```
